```python
import math
import jax, jax.numpy as jnp
from jax import lax
import numpy as np

D_MODEL = 2048
BATCH = 4
SEQ = 2048
DEPTH = 2
DEC_BATCH = 32
DEC_SEQ = 4
PAST_LEN = 8192
PAGE_SIZE = 128

N_BRANCH = 4
W_MIX = D_MODEL // 4
W_A = W_MIX
CONV_W = 3
H_B = 4
DH_B = W_MIX // H_B
FORGET_BIAS_INIT = 3.0
W_C = W_MIX
POOL_WINDOWS = (2, 4, 8, 16)
N_POOL = len(POOL_WINDOWS)
GC = W_C // N_POOL
POOL_BUF = max(POOL_WINDOWS) - 1
H_D = 4
DQK_D = W_MIX // (2 * H_D)
DV_D = 2 * DQK_D
SPLITS = (W_A, W_A, W_A, H_B * DH_B, H_B * DH_B, H_B * DH_B, H_B, W_C,
          H_D * 2 * DQK_D, H_D * 2 * DQK_D, H_D * DV_D)
N_IN = sum(SPLITS)
D_FF = 11 * D_MODEL // 4
N_EXP = 8
TOP_K = 2
D_EXP = 7 * D_MODEL // 2
N_DENSE = (DEPTH + 1) // 2
N_MOE = DEPTH // 2
Q_BLOCK = 128
EPS = 1e-6
NEG_INF = -1e30

kernel_name = "hybrid_gated_parallel_decoder_step"


def _rmsnorm(x, g):
    xf = x.astype(jnp.float32)
    y = xf * lax.rsqrt(jnp.mean(xf * xf, axis=-1, keepdims=True) + EPS)
    return (y * g.astype(jnp.float32)).astype(x.dtype)


def _sweep_queries(block_fn, T):
    qb = min(Q_BLOCK, T)
    nb = T // qb
    out = lax.map(lambda i: block_fn(i * qb, qb), jnp.arange(nb))
    out = jnp.moveaxis(out, 0, 1)
    return out.reshape((out.shape[0], T) + out.shape[3:])


def _token_mixers(h, past, lp, lam_init):
    (w_in, b_forget, conv_w, pool_w, pool_scale, diff_lambda, diff_subln,
     w_branch, w_gate, b_gate, w_o) = lp
    fox_kv_past, fox_logf_past, diff_kv_past, conv_buf, pool_buf = past
    f32 = jnp.float32
    B, T, _ = h.shape
    P = fox_kv_past.shape[1]
    q_pos = P + jnp.arange(T)
    k_pos = jnp.arange(P + T)

    proj = h @ w_in
    (a_x, a_b, a_c, f_q, f_k, f_v, f_f, c_x, d_q, d_k, d_v) = jnp.split(
        proj, np.cumsum(SPLITS)[:-1].tolist(), axis=-1)

    u = a_c * a_x
    u_ext = jnp.concatenate([conv_buf.astype(u.dtype), u], axis=1)
    conv = conv_w[0] * u_ext[:, 0:T]
    for j in range(1, CONV_W):
        conv = conv + conv_w[j] * u_ext[:, j:j + T]
    out_a = a_b * conv
    new_conv = u_ext[:, T:]

    fq = f_q.reshape(B, T, H_B, DH_B)
    fk = f_k.reshape(B, T, H_B, DH_B)
    fv = f_v.reshape(B, T, H_B, DH_B)
    logf = jax.nn.log_sigmoid((f_f + b_forget).astype(f32))
    k_all = jnp.concatenate([fox_kv_past[:, :, 0].astype(fk.dtype), fk], axis=1)
    v_all = jnp.concatenate([fox_kv_past[:, :, 1].astype(fv.dtype), fv], axis=1)
    F_all = jnp.cumsum(jnp.concatenate([fox_logf_past.astype(f32), logf], axis=1), axis=1)
    F_all_t = F_all.transpose(0, 2, 1)
    F_q = F_all[:, P:]

    def fox_block(i0, qb):
        qblk = lax.dynamic_slice_in_dim(fq, i0, qb, axis=1)
        Fqb = lax.dynamic_slice_in_dim(F_q, i0, qb, axis=1).transpose(0, 2, 1)
        qpb = lax.dynamic_slice_in_dim(q_pos, i0, qb)
        s = jnp.einsum('bqhd,bkhd->bhqk', qblk, k_all).astype(f32) * (DH_B ** -0.5)
        s = s + Fqb[..., None] - F_all_t[:, :, None, :]
        s = jnp.where(k_pos[None, :] <= qpb[:, None], s, NEG_INF)
        p = jax.nn.softmax(s, axis=-1)
        return jnp.einsum('bhqk,bkhd->bqhd', p.astype(v_all.dtype), v_all)

    out_b = _sweep_queries(fox_block, T).reshape(B, T, H_B * DH_B)
    new_fox_kv = jnp.stack([fk, fv], axis=2)
    new_logf = logf.astype(h.dtype)

    c_ext = jnp.concatenate([pool_buf.astype(c_x.dtype), c_x], axis=1)
    csum = jnp.concatenate([jnp.zeros((B, 1, W_C), f32),
                            jnp.cumsum(c_ext.astype(f32), axis=1)], axis=1)
    pooled = []
    for g, w in enumerate(POOL_WINDOWS):
        cg = csum[..., g * GC:(g + 1) * GC]
        win_sum = cg[:, POOL_BUF + 1:] - cg[:, POOL_BUF + 1 - w:POOL_BUF + 1 - w + T]
        cnt = jnp.minimum(w, q_pos + 1).astype(f32)
        pooled.append(win_sum / cnt[None, :, None])
    pooled = jnp.concatenate(pooled, axis=-1) - c_x.astype(f32)
    out_c = jnp.einsum('btgc,gcd->btgd', pooled.reshape(B, T, N_POOL, GC).astype(h.dtype),
                       pool_w).reshape(B, T, W_C) * pool_scale
    new_pool = c_ext[:, T:]

    dq = d_q.reshape(B, T, H_D, 2, DQK_D)
    dk = d_k.reshape(B, T, H_D, 2 * DQK_D)
    dv = d_v.reshape(B, T, H_D, DV_D)
    dk_all = jnp.concatenate([diff_kv_past[:, :, 0].astype(dk.dtype), dk], axis=1)
    dk_all = dk_all.reshape(B, P + T, H_D, 2, DQK_D)
    dv_all = jnp.concatenate([diff_kv_past[:, :, 1].astype(dv.dtype), dv], axis=1)
    dl = diff_lambda.astype(f32)
    lam = jnp.exp(jnp.sum(dl[0] * dl[1])) - jnp.exp(jnp.sum(dl[2] * dl[3])) + lam_init

    def diff_block(i0, qb):
        qblk = lax.dynamic_slice_in_dim(dq, i0, qb, axis=1)
        qpb = lax.dynamic_slice_in_dim(q_pos, i0, qb)
        s = jnp.einsum('bqhnd,bkhnd->nbhqk', qblk, dk_all).astype(f32) * (DQK_D ** -0.5)
        s = jnp.where(k_pos[None, :] <= qpb[:, None], s, NEG_INF)
        p = jax.nn.softmax(s, axis=-1)
        a = p[0] - lam * p[1]
        return jnp.einsum('bhqk,bkhd->bqhd', a.astype(dv_all.dtype), dv_all)

    o_d = _sweep_queries(diff_block, T)
    o_d = _rmsnorm(o_d, diff_subln) * (1.0 - lam_init)
    out_d = o_d.reshape(B, T, H_D * DV_D)
    new_diff_kv = jnp.stack([dk, dv], axis=2)

    branches = jnp.stack([out_a, out_b, out_c, out_d], axis=2)
    o = jnp.einsum('btnw,nwd->btnd', branches, w_branch)
    gates = jax.nn.sigmoid((h @ w_gate + b_gate).astype(f32)).reshape(B, T, N_BRANCH, D_MODEL)
    merged = jnp.sum(gates * o.astype(f32), axis=2).astype(h.dtype)
    y = merged @ w_o
    return y, (new_fox_kv, new_logf, new_diff_kv, new_conv, new_pool)


def _swiglu(h, w_in, w_out):
    g, u = jnp.split(h @ w_in, 2, axis=-1)
    return (jax.nn.silu(g) * u) @ w_out


def _moe(h, router, w_in, w_out):
    logits = (h @ router).astype(jnp.float32)
    top_v, top_i = lax.top_k(logits, TOP_K)
    top_w = jax.nn.softmax(top_v, axis=-1)
    comb = jnp.einsum('btk,btke->bte', top_w, jax.nn.one_hot(top_i, N_EXP, dtype=jnp.float32))
    y = jnp.zeros(h.shape, jnp.float32)
    for e in range(N_EXP):
        y = y + comb[..., e:e + 1] * _swiglu(h, w_in[e], w_out[e]).astype(jnp.float32)
    return y.astype(h.dtype)


def setup_inputs(seed: int = 0) -> dict:
    key = jax.random.key(seed)
    ks = jax.random.split(key, 32)
    nrm = jax.random.normal
    f32 = jnp.float32
    n_pages = PAST_LEN // PAGE_SIZE
    n_used = DEC_BATCH * n_pages
    n_phys = n_used + n_used // 4
    page_table = jax.random.permutation(ks[0], n_phys)[:n_used].reshape(DEC_BATCH, n_pages).astype(jnp.int32)
    D = D_MODEL
    return {
        "x_prompt": nrm(ks[1], (BATCH, SEQ, D), f32),
        "x_sample": nrm(ks[2], (DEC_BATCH, DEC_SEQ, D), f32),
        "cache_fox_kv": nrm(ks[3], (DEPTH, n_phys, PAGE_SIZE, 2, H_B, DH_B), f32),
        "cache_fox_logf": jax.nn.log_sigmoid(FORGET_BIAS_INIT + nrm(ks[4], (DEPTH, n_phys, PAGE_SIZE, H_B), f32)),
        "cache_diff_kv": nrm(ks[5], (DEPTH, n_phys, PAGE_SIZE, 2, H_D, DV_D), f32),
        "state_conv": nrm(ks[6], (DEPTH, DEC_BATCH, CONV_W - 1, W_A), f32),
        "state_pool": nrm(ks[7], (DEPTH, DEC_BATCH, POOL_BUF, W_C), f32),
        "page_table": page_table,
        "norm_mix": 1.0 + 0.02 * nrm(ks[8], (DEPTH, D), f32),
        "w_in": nrm(ks[9], (DEPTH, D, N_IN), f32) * D ** -0.5,
        "b_forget": FORGET_BIAS_INIT + 0.1 * nrm(ks[10], (DEPTH, H_B), f32),
        "conv_w": nrm(ks[11], (DEPTH, CONV_W, W_A), f32) * CONV_W ** -0.5,
        "pool_w": nrm(ks[12], (DEPTH, N_POOL, GC, GC), f32) * GC ** -0.5,
        "pool_scale": 1.0 + 0.02 * nrm(ks[13], (DEPTH, W_C), f32),
        "diff_lambda": 0.1 * nrm(ks[14], (DEPTH, 4, DQK_D), f32),
        "diff_subln": 1.0 + 0.02 * nrm(ks[15], (DEPTH, DV_D), f32),
        "w_branch": nrm(ks[16], (DEPTH, N_BRANCH, W_MIX, D), f32) * W_MIX ** -0.5,
        "w_gate": nrm(ks[17], (DEPTH, D, N_BRANCH * D), f32) * D ** -0.5,
        "b_gate": 0.02 * nrm(ks[18], (DEPTH, N_BRANCH * D), f32),
        "w_o": nrm(ks[19], (DEPTH, D, D), f32) * D ** -0.5,
        "norm_ffn": 1.0 + 0.02 * nrm(ks[20], (DEPTH, D), f32),
        "ffn_w_in": nrm(ks[21], (N_DENSE, D, 2 * D_FF), f32) * D ** -0.5,
        "ffn_w_out": nrm(ks[22], (N_DENSE, D_FF, D), f32) * D_FF ** -0.5,
        "moe_router": nrm(ks[23], (N_MOE, D, N_EXP), f32) * D ** -0.5,
        "moe_w_in": nrm(ks[24], (N_MOE, N_EXP, D, 2 * D_EXP), f32) * D ** -0.5,
        "moe_w_out": nrm(ks[25], (N_MOE, N_EXP, D_EXP, D), f32) * D_EXP ** -0.5,
        "norm_final": 1.0 + 0.02 * nrm(ks[26], (D,), f32),
    }


def reference(x_prompt, x_sample, cache_fox_kv, cache_fox_logf, cache_diff_kv, state_conv, state_pool,
              page_table, norm_mix, w_in, b_forget, conv_w, pool_w, pool_scale, diff_lambda, diff_subln,
              w_branch, w_gate, b_gate, w_o, norm_ffn, ffn_w_in, ffn_w_out, moe_router, moe_w_in,
              moe_w_out, norm_final):
    dt = x_prompt.dtype
    bp = x_prompt.shape[0]
    bs, n_pages = page_table.shape
    past_len = n_pages * PAGE_SIZE
    empty_past = (jnp.zeros((bp, 0, 2, H_B, DH_B), dt), jnp.zeros((bp, 0, H_B), dt),
                  jnp.zeros((bp, 0, 2, H_D, DV_D), dt), jnp.zeros((bp, CONV_W - 1, W_A), dt),
                  jnp.zeros((bp, POOL_BUF, W_C), dt))
    xp, xs = x_prompt, x_sample
    st_p, st_s = [], []
    for l in range(DEPTH):
        lp = (w_in[l], b_forget[l], conv_w[l], pool_w[l], pool_scale[l], diff_lambda[l], diff_subln[l],
              w_branch[l], w_gate[l], b_gate[l], w_o[l])
        lam_init = 0.8 - 0.6 * math.exp(-0.3 * l)
        sample_past = (
            cache_fox_kv[l, page_table].reshape(bs, past_len, 2, H_B, DH_B),
            cache_fox_logf[l, page_table].reshape(bs, past_len, H_B),
            cache_diff_kv[l, page_table].reshape(bs, past_len, 2, H_D, DV_D),
            state_conv[l], state_pool[l])
        yp, sp = _token_mixers(_rmsnorm(xp, norm_mix[l]), empty_past, lp, lam_init)
        ys, ss = _token_mixers(_rmsnorm(xs, norm_mix[l]), sample_past, lp, lam_init)
        xp = xp + yp
        xs = xs + ys
        hp = _rmsnorm(xp, norm_ffn[l])
        hs = _rmsnorm(xs, norm_ffn[l])
        j = l // 2
        if l % 2 == 0:
            xp = xp + _swiglu(hp, ffn_w_in[j], ffn_w_out[j])
            xs = xs + _swiglu(hs, ffn_w_in[j], ffn_w_out[j])
        else:
            xp = xp + _moe(hp, moe_router[j], moe_w_in[j], moe_w_out[j])
            xs = xs + _moe(hs, moe_router[j], moe_w_in[j], moe_w_out[j])
        st_p.append(sp)
        st_s.append(ss)
    y_prompt = _rmsnorm(xp, norm_final)
    y_sample = _rmsnorm(xs, norm_final)
    fox_kv_p = jnp.stack([s[0] for s in st_p])
    fox_logf_p = jnp.stack([s[1] for s in st_p])
    diff_kv_p = jnp.stack([s[2] for s in st_p])
    conv_p = jnp.stack([s[3] for s in st_p])
    pool_p = jnp.stack([s[4] for s in st_p])
    fox_kv_s = jnp.stack([s[0] for s in st_s])
    fox_logf_s = jnp.stack([s[1] for s in st_s])
    diff_kv_s = jnp.stack([s[2] for s in st_s])
    conv_s = jnp.stack([s[3] for s in st_s])
    pool_s = jnp.stack([s[4] for s in st_s])
    return (y_prompt, y_sample, fox_kv_p, fox_logf_p, diff_kv_p, conv_p, pool_p,
            fox_kv_s, fox_logf_s, diff_kv_s, conv_s, pool_s)
```

```python
import functools
import math

import jax
import jax.numpy as jnp
from jax import lax
from jax.experimental import pallas as pl
from jax.experimental.pallas import tpu as pltpu

F32 = jnp.float32
BF16 = jnp.bfloat16

D_MODEL = 2048
DEPTH = 2
PAGE_SIZE = 128
N_BRANCH = 4
W_MIX = D_MODEL // 4
CONV_W = 3
H_ATT = 4
DH = W_MIX // H_ATT
DQK_D = DH // 2
FOX_SCALE = DH ** -0.5
DIFF_SCALE = DQK_D ** -0.5
POOL_WINDOWS = (2, 4, 8, 16)
GC = W_MIX // len(POOL_WINDOWS)
POOL_BUF = max(POOL_WINDOWS) - 1
D_FF = 11 * D_MODEL // 4
N_EXP = 8
TOP_K = 2
D_EXP = 7 * D_MODEL // 2
EPS = 1e-6
NEG_INF = -1e30
C_AX, C_AB, C_AC, C_FQ, C_FK, C_FV, C_CX, C_DQ, C_DK, C_DV = (i * W_MIX for i in range(10))
N_PROJ = 10 * W_MIX
C_FF_ORIG = 6 * W_MIX

LANES = 128
SUBLANES = 8
VMEM_LIMIT = 56 * 1024 * 1024

TM = 640
T_SEQ = 512
TQ = 512
PAGES_PER_STEP = 8
TM_E = 256
TN_E = 1024


def _cparams(*sem):
    return pltpu.CompilerParams(dimension_semantics=sem, vmem_limit_bytes=VMEM_LIMIT)


def _nt_dot(a, b, **kw):
    return lax.dot_general(a, b, (((1,), (1,)), ((), ())), preferred_element_type=F32, **kw)


def _rms(x, g):
    return x * lax.rsqrt(jnp.mean(x * x, axis=-1, keepdims=True) + EPS) * g


def _log_sigmoid(z):
    return jnp.minimum(z, 0.0) - jnp.log1p(jnp.exp(-jnp.abs(z)))


def _norm_plain_body(x_ref, g_ref, h_ref):
    h_ref[...] = _rms(x_ref[...], g_ref[...]).astype(h_ref.dtype)


def _norm_plain(x, g, out_dtype):
    n, d = x.shape
    return pl.pallas_call(
        _norm_plain_body,
        out_shape=jax.ShapeDtypeStruct((n, d), out_dtype),
        grid=(n // TM,),
        in_specs=[pl.BlockSpec((TM, d), lambda i: (i, 0)),
                  pl.BlockSpec((1, d), lambda i: (0, 0))],
        out_specs=pl.BlockSpec((TM, d), lambda i: (i, 0)),
        compiler_params=_cparams("parallel"),
        name="norm_plain",
    )(x, g.reshape(1, d))


def _norm_forget_body(x_ref, g_ref, wf_ref, bf_ref, h_ref, lf_ref):
    hb = _rms(x_ref[...], g_ref[...]).astype(BF16)
    h_ref[...] = hb
    lf_ref[...] = _log_sigmoid(_nt_dot(wf_ref[...], hb) + bf_ref[...])


def _norm_forget(x, g, wf_t, bf):
    n, d = x.shape
    return pl.pallas_call(
        _norm_forget_body,
        out_shape=(jax.ShapeDtypeStruct((n, d), BF16), jax.ShapeDtypeStruct((SUBLANES, n), F32)),
        grid=(n // TM,),
        in_specs=[pl.BlockSpec((TM, d), lambda i: (i, 0)),
                  pl.BlockSpec((1, d), lambda i: (0, 0)),
                  pl.BlockSpec((SUBLANES, d), lambda i: (0, 0)),
                  pl.BlockSpec((SUBLANES, 1), lambda i: (0, 0))],
        out_specs=(pl.BlockSpec((TM, d), lambda i: (i, 0)),
                   pl.BlockSpec((SUBLANES, TM), lambda i: (0, i))),
        compiler_params=_cparams("parallel"),
        name="norm_forget",
    )(x, g.reshape(1, d), wf_t, bf)


def _norm_route_body(x_ref, g_ref, wr_ref, h_ref, ids_ref, wts_ref):
    h = _rms(x_ref[...], g_ref[...])
    h_ref[...] = h.astype(BF16)
    logits = jnp.dot(h, wr_ref[...], preferred_element_type=F32, precision=lax.Precision.HIGHEST)
    lane = lax.broadcasted_iota(jnp.int32, logits.shape, 1)
    logits = jnp.where(lane < N_EXP, logits, -jnp.inf)
    m1 = jnp.max(logits, axis=-1, keepdims=True)
    i1 = jnp.min(jnp.where(logits == m1, lane, LANES), axis=-1, keepdims=True)
    rest = jnp.where(lane == i1, -jnp.inf, logits)
    m2 = jnp.max(rest, axis=-1, keepdims=True)
    i2 = jnp.min(jnp.where(rest == m2, lane, LANES), axis=-1, keepdims=True)
    e2 = jnp.exp(m2 - m1)
    w1 = 1.0 / (1.0 + e2)
    w2 = e2 / (1.0 + e2)
    ids_ref[...] = jnp.where(lane == 0, i1, i2)
    wts_ref[...] = jnp.where(lane == 0, w1, w2)


def _norm_route(x, g, router_pad):
    n, d = x.shape
    return pl.pallas_call(
        _norm_route_body,
        out_shape=(jax.ShapeDtypeStruct((n, d), BF16),
                   jax.ShapeDtypeStruct((n, LANES), jnp.int32),
                   jax.ShapeDtypeStruct((n, LANES), F32)),
        grid=(n // TM,),
        in_specs=[pl.BlockSpec((TM, d), lambda i: (i, 0)),
                  pl.BlockSpec((1, d), lambda i: (0, 0)),
                  pl.BlockSpec((d, LANES), lambda i: (0, 0))],
        out_specs=(pl.BlockSpec((TM, d), lambda i: (i, 0)),
                   pl.BlockSpec((TM, LANES), lambda i: (i, 0)),
                   pl.BlockSpec((TM, LANES), lambda i: (i, 0))),
        compiler_params=_cparams("parallel"),
        name="norm_route",
    )(x, g.reshape(1, d), router_pad)


def _mm_body(x_ref, w_ref, o_ref):
    o_ref[...] = jnp.dot(x_ref[...], w_ref[...], preferred_element_type=F32).astype(o_ref.dtype)


def _mm_res_body(x_ref, w_ref, r_ref, o_ref):
    o_ref[...] = r_ref[...] + jnp.dot(x_ref[...], w_ref[...], preferred_element_type=F32)


def _matmul(x, w, tn, res=None, out_dtype=F32):
    m, k = x.shape
    n = w.shape[1]
    in_specs = [pl.BlockSpec((TM, k), lambda j, i: (i, 0)),
                pl.BlockSpec((k, tn), lambda j, i: (0, j))]
    args = [x, w]
    body = _mm_body
    if res is not None:
        in_specs.append(pl.BlockSpec((TM, tn), lambda j, i: (i, j)))
        args.append(res)
        body = _mm_res_body
    return pl.pallas_call(
        body,
        out_shape=jax.ShapeDtypeStruct((m, n), out_dtype),
        grid=(n // tn, m // TM),
        in_specs=in_specs,
        out_specs=pl.BlockSpec((TM, tn), lambda j, i: (i, j)),
        compiler_params=_cparams("parallel", "parallel"),
        name="matmul",
    )(*args)


def _swiglu_in_body(x_ref, wg_ref, wu_ref, o_ref):
    x = x_ref[...]
    g = jnp.dot(x, wg_ref[...], preferred_element_type=F32)
    u = jnp.dot(x, wu_ref[...], preferred_element_type=F32)
    o_ref[...] = (g * jax.nn.sigmoid(g) * u).astype(o_ref.dtype)


def _swiglu_in(x, w_in, tn):
    m, k = x.shape
    f = w_in.shape[1] // 2
    nj = f // tn
    return pl.pallas_call(
        _swiglu_in_body,
        out_shape=jax.ShapeDtypeStruct((m, f), BF16),
        grid=(nj, m // TM),
        in_specs=[pl.BlockSpec((TM, k), lambda j, i: (i, 0)),
                  pl.BlockSpec((k, tn), lambda j, i: (0, j)),
                  pl.BlockSpec((k, tn), lambda j, i: (0, nj + j))],
        out_specs=pl.BlockSpec((TM, tn), lambda j, i: (i, j)),
        compiler_params=_cparams("parallel", "parallel"),
        name="swiglu_in",
    )(x, w_in, w_in)


def _merge_body(h_ref, ba_ref, bb_ref, bc_ref, bd_ref, wg0, wg1, wg2, wg3, wb0, wb1, wb2, wb3,
                bg0, bg1, bg2, bg3, o_ref):
    h = h_ref[...]
    acc = None
    for br_ref, wg_ref, wb_ref, bg_ref in ((ba_ref, wg0, wb0, bg0), (bb_ref, wg1, wb1, bg1),
                                           (bc_ref, wg2, wb2, bg2), (bd_ref, wg3, wb3, bg3)):
        gate = jax.nn.sigmoid(jnp.dot(h, wg_ref[...], preferred_element_type=F32) + bg_ref[...])
        o = jnp.dot(br_ref[...], wb_ref[0], preferred_element_type=F32)
        acc = gate * o if acc is None else acc + gate * o
    o_ref[...] = acc.astype(o_ref.dtype)


def _merge(h, branches, w_gate, w_branch, b_gate, tn):
    m, d = h.shape
    nj = d // tn
    tok = lambda j, i: (i, 0)
    in_specs = [pl.BlockSpec((TM, d), tok)]
    in_specs += [pl.BlockSpec((TM, W_MIX), tok) for _ in range(N_BRANCH)]
    in_specs += [pl.BlockSpec((d, tn), functools.partial(lambda n, j, i: (0, n * nj + j), n))
                 for n in range(N_BRANCH)]
    in_specs += [pl.BlockSpec((1, W_MIX, tn), functools.partial(lambda n, j, i: (n, 0, j), n))
                 for n in range(N_BRANCH)]
    in_specs += [pl.BlockSpec((1, tn), functools.partial(lambda n, j, i: (0, n * nj + j), n))
                 for n in range(N_BRANCH)]
    return pl.pallas_call(
        _merge_body,
        out_shape=jax.ShapeDtypeStruct((m, d), BF16),
        grid=(nj, m // TM),
        in_specs=in_specs,
        out_specs=pl.BlockSpec((TM, tn), lambda j, i: (i, j)),
        compiler_params=_cparams("parallel", "parallel"),
        name="merge",
    )(h, *branches, *([w_gate] * N_BRANCH), *([w_branch] * N_BRANCH), *([b_gate] * N_BRANCH))


def _conv_prompt_body(ax_ref, ab_ref, ac_ref, w_ref, o_ref, st_ref, ext_ref, *, tt):
    i = pl.program_id(1)

    @pl.when(i == 0)
    def _():
        ext_ref[0:SUBLANES, :] = jnp.zeros((SUBLANES, W_MIX), F32)

    u = ac_ref[...] * ax_ref[...]
    ext_ref[SUBLANES:, :] = u
    w = w_ref[...]
    conv = (w[0:1] * ext_ref[SUBLANES - 2:SUBLANES - 2 + tt, :]
            + w[1:2] * ext_ref[SUBLANES - 1:SUBLANES - 1 + tt, :] + w[2:3] * u)
    o_ref[...] = (ab_ref[...] * conv).astype(o_ref.dtype)
    ext_ref[0:SUBLANES, :] = ext_ref[tt:tt + SUBLANES, :]

    @pl.when(i == pl.num_programs(1) - 1)
    def _():
        st_ref[0] = ext_ref[tt + SUBLANES - (CONV_W - 1):tt + SUBLANES, :]


def _conv_prompt(proj, conv_w, nb, t):
    tt = T_SEQ
    nt = t // tt
    cb = lambda c: (lambda b, i: (b * nt + i, c // W_MIX))
    return pl.pallas_call(
        functools.partial(_conv_prompt_body, tt=tt),
        out_shape=(jax.ShapeDtypeStruct((nb * t, W_MIX), BF16),
                   jax.ShapeDtypeStruct((nb, CONV_W - 1, W_MIX), F32)),
        grid=(nb, nt),
        in_specs=[pl.BlockSpec((tt, W_MIX), cb(C_AX)),
                  pl.BlockSpec((tt, W_MIX), cb(C_AB)),
                  pl.BlockSpec((tt, W_MIX), cb(C_AC)),
                  pl.BlockSpec((CONV_W, W_MIX), lambda b, i: (0, 0))],
        out_specs=(pl.BlockSpec((tt, W_MIX), lambda b, i: (b * nt + i, 0)),
                   pl.BlockSpec((1, CONV_W - 1, W_MIX), lambda b, i: (b, 0, 0))),
        scratch_shapes=[pltpu.VMEM((tt + SUBLANES, W_MIX), F32)],
        compiler_params=_cparams("parallel", "arbitrary"),
        name="conv_prompt",
    )(proj, proj, proj, conv_w)


def _conv_sample_body(ax_ref, ab_ref, ac_ref, st_ref, w_ref, o_ref, nst_ref, *, t):
    w = w_ref[...]
    u = [ac_ref[j] * ax_ref[j] for j in range(t)]
    ext = [st_ref[j] for j in range(CONV_W - 1)] + u
    for j in range(t):
        conv = w[0:1] * ext[j]
        for c in range(1, CONV_W):
            conv = conv + w[c:c + 1] * ext[j + c]
        o_ref[j] = ab_ref[j] * conv
    for j in range(CONV_W - 1):
        nst_ref[j] = ext[t + j]


def _conv_sample(ax, ab, ac, state, conv_w):
    t, nb, _ = ax.shape
    return pl.pallas_call(
        functools.partial(_conv_sample_body, t=t),
        out_shape=(jax.ShapeDtypeStruct((t, nb, W_MIX), F32),
                   jax.ShapeDtypeStruct((CONV_W - 1, nb, W_MIX), F32)),
        name="conv_sample",
    )(ax, ab, ac, state, conv_w)


def _pool_prompt_body(c_ref, pw_ref, ps_ref, o_ref, st_ref, ext_ref, *, tt):
    i = pl.program_id(1)
    hist = 2 * SUBLANES

    @pl.when(i == 0)
    def _():
        ext_ref[0:hist, :] = jnp.zeros((hist, W_MIX), F32)

    c = c_ref[...]
    ext_ref[hist:, :] = c
    pos = i * tt + lax.broadcasted_iota(jnp.int32, (tt, 1), 0)
    for g, win in enumerate(POOL_WINDOWS):
        cols = slice(g * GC, (g + 1) * GC)
        cg = c[:, cols]
        s = cg
        for j in range(1, win):
            s = s + ext_ref[hist - j:hist - j + tt, cols]
        cnt = jnp.minimum(win, pos + 1).astype(F32)
        pooled = s / cnt - cg
        og = jnp.dot(pooled.astype(BF16), pw_ref[g], preferred_element_type=F32)
        o_ref[:, cols] = (og * ps_ref[:, cols]).astype(o_ref.dtype)
    ext_ref[0:hist, :] = ext_ref[tt:tt + hist, :]

    @pl.when(i == pl.num_programs(1) - 1)
    def _():
        st_ref[0] = ext_ref[tt + hist - POOL_BUF:tt + hist, :]


def _pool_prompt(proj, pool_w, pool_scale, nb, t):
    tt = T_SEQ
    nt = t // tt
    return pl.pallas_call(
        functools.partial(_pool_prompt_body, tt=tt),
        out_shape=(jax.ShapeDtypeStruct((nb * t, W_MIX), BF16),
                   jax.ShapeDtypeStruct((nb, POOL_BUF, W_MIX), F32)),
        grid=(nb, nt),
        in_specs=[pl.BlockSpec((tt, W_MIX), lambda b, i: (b * nt + i, C_CX // W_MIX)),
                  pl.BlockSpec((len(POOL_WINDOWS), GC, GC), lambda b, i: (0, 0, 0)),
                  pl.BlockSpec((1, W_MIX), lambda b, i: (0, 0))],
        out_specs=(pl.BlockSpec((tt, W_MIX), lambda b, i: (b * nt + i, 0)),
                   pl.BlockSpec((1, POOL_BUF, W_MIX), lambda b, i: (b, 0, 0))),
        scratch_shapes=[pltpu.VMEM((tt + 2 * SUBLANES, W_MIX), F32)],
        compiler_params=_cparams("parallel", "arbitrary"),
        name="pool_prompt",
    )(proj, pool_w, pool_scale)


def _pool_sample_body(c_ref, st_ref, pw_ref, ps_ref, o_ref, nst_ref, *, t, past_len):
    ext = [st_ref[j] for j in range(POOL_BUF)] + [c_ref[j] for j in range(t)]
    for g, win in enumerate(POOL_WINDOWS):
        cols = slice(g * GC, (g + 1) * GC)
        rows = []
        for j in range(t):
            s = ext[POOL_BUF + j][:, cols]
            for k in range(1, win):
                s = s + ext[POOL_BUF + j - k][:, cols]
            cnt = float(min(win, past_len + j + 1))
            rows.append(s / cnt - ext[POOL_BUF + j][:, cols])
        pooled = jnp.concatenate(rows, axis=0)
        og = jnp.dot(pooled.astype(BF16), pw_ref[g], preferred_element_type=F32) * ps_ref[:, cols]
        nb = og.shape[0] // t
        for j in range(t):
            o_ref[j, :, cols] = og[j * nb:(j + 1) * nb]
    for j in range(POOL_BUF):
        nst_ref[j] = ext[t + j]


def _pool_sample(c, state, pool_w, pool_scale, past_len):
    t, nb, _ = c.shape
    return pl.pallas_call(
        functools.partial(_pool_sample_body, t=t, past_len=past_len),
        out_shape=(jax.ShapeDtypeStruct((t, nb, W_MIX), F32),
                   jax.ShapeDtypeStruct((POOL_BUF, nb, W_MIX), F32)),
        name="pool_sample",
    )(c, state, pool_w, pool_scale)


def _upper_ones(n):
    r = lax.broadcasted_iota(jnp.int32, (n, n), 0)
    c = lax.broadcasted_iota(jnp.int32, (n, n), 1)
    return (r <= c).astype(F32)


def _cumsum_body(lf_ref, o_ref, carry_ref, *, tc):
    @pl.when(pl.program_id(1) == 0)
    def _():
        carry_ref[...] = jnp.zeros_like(carry_ref)

    f = jnp.dot(lf_ref[...], _upper_ones(tc), preferred_element_type=F32,
                precision=lax.Precision.HIGHEST) + carry_ref[...]
    o_ref[...] = f
    carry_ref[...] = f[:, tc - 1:tc]


def _cumsum_prompt(lf_t, nb, t):
    tc = T_SEQ
    nt = t // tc
    return pl.pallas_call(
        functools.partial(_cumsum_body, tc=tc),
        out_shape=jax.ShapeDtypeStruct((SUBLANES, nb * t), F32),
        grid=(nb, nt),
        in_specs=[pl.BlockSpec((SUBLANES, tc), lambda b, i: (0, b * nt + i))],
        out_specs=pl.BlockSpec((SUBLANES, tc), lambda b, i: (0, b * nt + i)),
        scratch_shapes=[pltpu.VMEM((SUBLANES, 1), F32)],
        compiler_params=_cparams("parallel", "arbitrary"),
        name="cumsum_prompt",
    )(lf_t)


def _diff_lambda(dl_ref, lam_init):
    dl = dl_ref[...]
    a = jnp.sum(dl[0:1] * dl[1:2], axis=-1, keepdims=True)
    b = jnp.sum(dl[2:3] * dl[3:4], axis=-1, keepdims=True)
    return jnp.exp(a) - jnp.exp(b) + lam_init


def _half_masks(shape):
    lane = lax.broadcasted_iota(jnp.int32, shape, len(shape) - 1)
    return lane < DQK_D, lane >= DQK_D


def _softmax_step(s, v, m_ref, l_ref, acc_ref, idx):
    m_prev = m_ref[idx]
    m_new = jnp.maximum(m_prev, jnp.max(s, axis=-1, keepdims=True))
    alpha = jnp.exp(m_prev - m_new)
    p = jnp.exp(s - m_new)
    l_ref[idx] = alpha * l_ref[idx] + jnp.sum(p, axis=-1, keepdims=True)
    acc_ref[idx] = alpha * acc_ref[idx] + jnp.dot(p.astype(BF16), v, preferred_element_type=F32)
    m_ref[idx] = m_new


def _attn_prompt_body(*refs, diff, lam_init, tq):
    if diff:
        q_ref, k_ref, v_ref, dl_ref, sub_ref, o_ref, m_ref, l_ref, acc_ref = refs
    else:
        q_ref, k_ref, v_ref, f_ref, o_ref, m_ref, l_ref, acc_ref = refs
    h = pl.program_id(1)
    qi = pl.program_id(2)
    ki = pl.program_id(3)
    n_soft = 2 if diff else 1

    @pl.when(ki == 0)
    def _():
        m_ref[...] = jnp.full(m_ref.shape, NEG_INF, F32)
        l_ref[...] = jnp.zeros(l_ref.shape, F32)
        acc_ref[...] = jnp.zeros(acc_ref.shape, F32)

    @pl.when(ki <= qi)
    def _():
        q = q_ref[...]
        k = k_ref[...].astype(BF16)
        v = v_ref[...].astype(BF16)
        row = qi * tq + lax.broadcasted_iota(jnp.int32, (tq, tq), 0)
        col = ki * tq + lax.broadcasted_iota(jnp.int32, (tq, tq), 1)
        causal = col <= row
        if diff:
            lo, hi = _half_masks(q.shape)
            qs = (jnp.where(lo, q, 0.0), jnp.where(hi, q, 0.0))
            scale = DIFF_SCALE
        else:
            qs = (q,)
            scale = FOX_SCALE
        for n in range(n_soft):
            s = _nt_dot(qs[n].astype(BF16), k) * scale
            if not diff:
                s = s - f_ref[pl.ds(h, 1), :]
            s = jnp.where(causal, s, NEG_INF)
            _softmax_step(s, v, m_ref, l_ref, acc_ref, n)

    @pl.when(ki == qi)
    def _():
        if diff:
            lam = _diff_lambda(dl_ref, lam_init)
            o = acc_ref[0] / l_ref[0] - lam * (acc_ref[1] / l_ref[1])
            o = _rms(o, sub_ref[...]) * (1.0 - lam_init)
        else:
            o = acc_ref[0] / l_ref[0]
        o_ref[...] = o.astype(o_ref.dtype)


def _attn_prompt(proj, nb, t, c_q, c_k, c_v, *, f_cum=None, diff_lambda=None, diff_subln=None,
                 lam_init=0.0):
    diff = f_cum is None
    tq = TQ
    nt = t // tq
    n_soft = 2 if diff else 1
    in_specs = [pl.BlockSpec((tq, DH), lambda b, h, qi, ki: (b * nt + qi, c_q // DH + h)),
                pl.BlockSpec((tq, DH), lambda b, h, qi, ki: (b * nt + jnp.minimum(ki, qi), c_k // DH + h)),
                pl.BlockSpec((tq, DH), lambda b, h, qi, ki: (b * nt + jnp.minimum(ki, qi), c_v // DH + h))]
    args = [proj, proj, proj]
    if diff:
        in_specs += [pl.BlockSpec((4, DQK_D), lambda b, h, qi, ki: (0, 0)),
                     pl.BlockSpec((1, DH), lambda b, h, qi, ki: (0, 0))]
        args += [diff_lambda, diff_subln.reshape(1, DH)]
    else:
        in_specs += [pl.BlockSpec((SUBLANES, tq), lambda b, h, qi, ki: (0, b * nt + jnp.minimum(ki, qi)))]
        args += [f_cum]
    return pl.pallas_call(
        functools.partial(_attn_prompt_body, diff=diff, lam_init=lam_init, tq=tq),
        out_shape=jax.ShapeDtypeStruct((nb * t, H_ATT * DH), BF16),
        grid=(nb, H_ATT, nt, nt),
        in_specs=in_specs,
        out_specs=pl.BlockSpec((tq, DH), lambda b, h, qi, ki: (b * nt + qi, h)),
        scratch_shapes=[pltpu.VMEM((n_soft, tq, 1), F32), pltpu.VMEM((n_soft, tq, 1), F32),
                        pltpu.VMEM((n_soft, tq, DH), F32)],
        compiler_params=_cparams("parallel", "parallel", "parallel", "arbitrary"),
        name="attn_prompt_diff" if diff else "attn_prompt_fox",
    )(*args)


def _attn_sample_body(*refs, diff, lam_init, t):
    np_ = PAGES_PER_STEP
    pt_ref, q_ref, kn_ref, vn_ref = refs[:4]
    if diff:
        dl_ref, sub_ref = refs[4:6]
        pages = refs[6:6 + np_]
        o_ref, m_ref, l_ref, acc_ref = refs[6 + np_:]
    else:
        lfn_ref = refs[4]
        pages = refs[5:5 + np_]
        lf_pages = refs[5 + np_:5 + 2 * np_]
        o_ref, m_ref, l_ref, acc_ref, carry_ref = refs[5 + 2 * np_:]
    del pt_ref
    step = pl.program_id(1)
    last = pl.num_programs(1) - 1
    r = 2 * t if diff else t
    scale = DIFF_SCALE if diff else FOX_SCALE

    @pl.when(step == 0)
    def _():
        m_ref[...] = jnp.full(m_ref.shape, NEG_INF, F32)
        l_ref[...] = jnp.zeros(l_ref.shape, F32)
        acc_ref[...] = jnp.zeros(acc_ref.shape, F32)
        if not diff:
            carry_ref[...] = jnp.zeros(carry_ref.shape, F32)

    q_all = q_ref[0]

    def head_q(h):
        qh = q_all[:, h * DH:(h + 1) * DH]
        if diff:
            lo, hi = _half_masks(qh.shape)
            qh = jnp.concatenate([jnp.where(lo, qh, 0.0), jnp.where(hi, qh, 0.0)], axis=0)
        return qh

    if not diff:
        tri = _upper_ones(PAGE_SIZE)
    for h in range(H_ATT):
        qb = head_q(h).astype(BF16)
        if not diff:
            lf = jnp.concatenate([lp[pl.ds(h, 1), :] for lp in lf_pages], axis=0)
            f_in = jnp.dot(lf, tri, preferred_element_type=F32, precision=lax.Precision.HIGHEST)
            off = carry_ref[h]
        parts = []
        for i in range(np_):
            kh = pages[i][:, h * DH:(h + 1) * DH].astype(BF16)
            sp = _nt_dot(qb, kh) * scale
            if not diff:
                sp = sp - (f_in[i:i + 1, :] + off)
                off = off + f_in[i:i + 1, PAGE_SIZE - 1:PAGE_SIZE]
            parts.append(sp)
        if not diff:
            carry_ref[h] = off
        s = jnp.concatenate(parts, axis=1)
        m_prev = m_ref[h]
        m_new = jnp.maximum(m_prev, jnp.max(s, axis=-1, keepdims=True))
        alpha = jnp.exp(m_prev - m_new)
        p = jnp.exp(s - m_new)
        l_ref[h] = alpha * l_ref[h] + jnp.sum(p, axis=-1, keepdims=True)
        pv = None
        for i in range(np_):
            vh = pages[i][:, (H_ATT + h) * DH:(H_ATT + h + 1) * DH].astype(BF16)
            d = jnp.dot(p[:, i * PAGE_SIZE:(i + 1) * PAGE_SIZE].astype(BF16), vh,
                        preferred_element_type=F32)
            pv = d if pv is None else pv + d
        acc_ref[h] = alpha * acc_ref[h] + pv
        m_ref[h] = m_new

    @pl.when(step == last)
    def _():
        qrow = lax.broadcasted_iota(jnp.int32, (r, 1), 0) % t
        if diff:
            lam = _diff_lambda(dl_ref, lam_init)
        for h in range(H_ATT):
            qh = head_q(h)
            kn = kn_ref[0][:, h * DH:(h + 1) * DH]
            vn = vn_ref[0][:, h * DH:(h + 1) * DH]
            m = m_ref[h]
            l = l_ref[h]
            acc = acc_ref[h]
            if not diff:
                off = carry_ref[h]
            for j in range(t):
                sj = jnp.sum(qh * kn[j:j + 1, :], axis=-1, keepdims=True) * scale
                if not diff:
                    off = off + lfn_ref[0][h:h + 1, j:j + 1]
                    sj = sj - off
                sj = jnp.where(qrow >= j, sj, NEG_INF)
                m_new = jnp.maximum(m, sj)
                alpha = jnp.exp(m - m_new)
                pj = jnp.exp(sj - m_new)
                l = alpha * l + pj
                acc = alpha * acc + pj * vn[j:j + 1, :]
                m = m_new
            o = acc / l
            if diff:
                o = o[0:t] - lam * o[t:2 * t]
                o = _rms(o, sub_ref[...]) * (1.0 - lam_init)
            o_ref[0, :, h * DH:(h + 1) * DH] = o


def _attn_sample(proj_s, cache, page_table, layer, c_q, c_k, c_v, *, lf_new=None, cache_lf=None,
                 diff_lambda=None, diff_subln=None, lam_init=0.0):
    diff = cache_lf is None
    nb, t, _ = proj_s.shape
    n_pages = page_table.shape[1]
    np_ = PAGES_PER_STEP
    n_steps = n_pages // np_
    wide = H_ATT * DH
    r = 2 * t if diff else t

    def page_map(i):
        return lambda b, s, pt: (layer, pt[b * n_pages + s * np_ + i], 0, 0)

    in_specs = [pl.BlockSpec((1, t, wide), lambda b, s, pt: (b, 0, c_q // wide)),
                pl.BlockSpec((1, t, wide), lambda b, s, pt: (b, 0, c_k // wide)),
                pl.BlockSpec((1, t, wide), lambda b, s, pt: (b, 0, c_v // wide))]
    args = [proj_s, proj_s, proj_s]
    if diff:
        in_specs += [pl.BlockSpec((4, DQK_D), lambda b, s, pt: (0, 0)),
                     pl.BlockSpec((1, DH), lambda b, s, pt: (0, 0))]
        args += [diff_lambda, diff_subln.reshape(1, DH)]
    else:
        in_specs += [pl.BlockSpec((1, SUBLANES, t), lambda b, s, pt: (b, 0, 0))]
        args += [lf_new]
    in_specs += [pl.BlockSpec((None, None, PAGE_SIZE, 2 * wide), page_map(i)) for i in range(np_)]
    args += [cache] * np_
    scratch = [pltpu.VMEM((H_ATT, r, 1), F32), pltpu.VMEM((H_ATT, r, 1), F32),
               pltpu.VMEM((H_ATT, r, DH), F32)]
    if not diff:
        in_specs += [pl.BlockSpec((None, None, H_ATT, PAGE_SIZE), page_map(i)) for i in range(np_)]
        args += [cache_lf] * np_
        scratch += [pltpu.VMEM((H_ATT, 1, 1), F32)]
    return pl.pallas_call(
        functools.partial(_attn_sample_body, diff=diff, lam_init=lam_init, t=t),
        out_shape=jax.ShapeDtypeStruct((nb, t, wide), F32),
        grid_spec=pltpu.PrefetchScalarGridSpec(
            num_scalar_prefetch=1,
            grid=(nb, n_steps),
            in_specs=in_specs,
            out_specs=pl.BlockSpec((1, t, wide), lambda b, s, pt: (b, 0, 0)),
            scratch_shapes=scratch),
        compiler_params=_cparams("parallel", "arbitrary"),
        name="attn_sample_diff" if diff else "attn_sample_fox",
    )(page_table.reshape(-1), *args)


def _moe_in_body(te_ref, nt_ref, x_ref, wg_ref, wu_ref, o_ref):
    del te_ref

    @pl.when(pl.program_id(1) < nt_ref[0])
    def _():
        x = x_ref[...]
        g = jnp.dot(x, wg_ref[0], preferred_element_type=F32)
        u = jnp.dot(x, wu_ref[0], preferred_element_type=F32)
        o_ref[...] = (g * jax.nn.sigmoid(g) * u).astype(o_ref.dtype)

    @pl.when(pl.program_id(1) >= nt_ref[0])
    def _():
        o_ref[...] = jnp.zeros(o_ref.shape, o_ref.dtype)


def _moe_in(xs, w_in, tile_expert, n_tiles):
    p, d = xs.shape
    f = w_in.shape[2] // 2
    nj = f // TN_E
    return pl.pallas_call(
        _moe_in_body,
        out_shape=jax.ShapeDtypeStruct((p, f), BF16),
        grid_spec=pltpu.PrefetchScalarGridSpec(
            num_scalar_prefetch=2,
            grid=(nj, p // TM_E),
            in_specs=[pl.BlockSpec((TM_E, d), lambda j, i, te, nt: (i, 0)),
                      pl.BlockSpec((1, d, TN_E), lambda j, i, te, nt: (te[i], 0, j)),
                      pl.BlockSpec((1, d, TN_E), lambda j, i, te, nt: (te[i], 0, nj + j))],
            out_specs=pl.BlockSpec((TM_E, TN_E), lambda j, i, te, nt: (i, j))),
        compiler_params=_cparams("parallel", "arbitrary"),
        name="moe_in",
    )(tile_expert, n_tiles, xs, w_in, w_in)


def _moe_out_body(te_ref, nt_ref, a_ref, w_ref, ws_ref, o_ref):
    del te_ref

    @pl.when(pl.program_id(1) < nt_ref[0])
    def _():
        o_ref[...] = jnp.dot(a_ref[...], w_ref[0], preferred_element_type=F32) * ws_ref[...]

    @pl.when(pl.program_id(1) >= nt_ref[0])
    def _():
        o_ref[...] = jnp.zeros(o_ref.shape, o_ref.dtype)


def _moe_out(act, w_out, w_slot, tile_expert, n_tiles, tn):
    p, f = act.shape
    d = w_out.shape[2]
    return pl.pallas_call(
        _moe_out_body,
        out_shape=jax.ShapeDtypeStruct((p, d), F32),
        grid_spec=pltpu.PrefetchScalarGridSpec(
            num_scalar_prefetch=2,
            grid=(d // tn, p // TM_E),
            in_specs=[pl.BlockSpec((TM_E, f), lambda j, i, te, nt: (i, 0)),
                      pl.BlockSpec((1, f, tn), lambda j, i, te, nt: (te[i], 0, j)),
                      pl.BlockSpec((TM_E, 1), lambda j, i, te, nt: (i, 0))],
            out_specs=pl.BlockSpec((TM_E, tn), lambda j, i, te, nt: (i, j))),
        compiler_params=_cparams("parallel", "arbitrary"),
        name="moe_out",
    )(tile_expert, n_tiles, act, w_out, w_slot)


def _combine_body(x_ref, g_ref, o_ref):
    o_ref[...] = x_ref[...] + g_ref[:, 0, :] + g_ref[:, 1, :]


def _combine(x, g):
    n, d = x.shape
    tm = TM // 2
    return pl.pallas_call(
        _combine_body,
        out_shape=jax.ShapeDtypeStruct((n, d), F32),
        grid=(n // tm,),
        in_specs=[pl.BlockSpec((tm, d), lambda i: (i, 0)),
                  pl.BlockSpec((tm, TOP_K, d), lambda i: (i, 0, 0))],
        out_specs=pl.BlockSpec((tm, d), lambda i: (i, 0)),
        compiler_params=_cparams("parallel"),
        name="moe_combine",
    )(x, g)


def _moe(x, g_norm, router, w_in, w_out):
    n, d = x.shape
    router_pad = jnp.pad(router, ((0, 0), (0, LANES - N_EXP)))
    h, ids, wts = _norm_route(x, g_norm, router_pad)
    n_slots = n * TOP_K
    p_total = n_slots + N_EXP * TM_E
    flat_e = ids[:, :TOP_K].reshape(-1)
    onehot = (flat_e[:, None] == jnp.arange(N_EXP, dtype=jnp.int32)[None, :]).astype(jnp.int32)
    rank = jnp.sum((jnp.cumsum(onehot, axis=0) - onehot) * onehot, axis=1)
    counts = jnp.sum(onehot, axis=0)
    padded = (counts + TM_E - 1) // TM_E * TM_E
    ends = jnp.cumsum(padded)
    dest = (ends - padded)[flat_e] + rank
    src_tok = jnp.zeros((p_total,), jnp.int32).at[dest].set(jnp.arange(n_slots, dtype=jnp.int32) // TOP_K)
    w_slot = jnp.zeros((p_total,), F32).at[dest].set(wts[:, :TOP_K].reshape(-1))
    n_tiles = (ends[-1] // TM_E).astype(jnp.int32).reshape(1)
    tile_start = jnp.arange(p_total // TM_E, dtype=jnp.int32) * TM_E
    tile_expert = jnp.minimum(jnp.searchsorted(ends, tile_start, side="right"), N_EXP - 1).astype(jnp.int32)
    last_valid = jnp.maximum(n_tiles[0] - 1, 0)
    tile_expert = jnp.where(tile_start // TM_E < n_tiles[0], tile_expert, tile_expert[last_valid])

    xs = jnp.take(h, src_tok, axis=0)
    act = _moe_in(xs, w_in, tile_expert, n_tiles)
    ys = _moe_out(act, w_out, w_slot.reshape(p_total, 1), tile_expert, n_tiles, 512)
    g = jnp.take(ys, dest.reshape(n, TOP_K), axis=0)
    return _combine(x, g)


def _token_mixers(x, layer, lam_init, nb_p, t_p, nb_s, t_s, caches, states, page_table, lp):
    (g_mix, w_in, b_forget, conv_w, pool_w, pool_scale, diff_lambda, diff_subln,
     w_branch, w_gate, b_gate, w_o) = lp
    cache_fox, cache_lf_t, cache_diff = caches
    state_conv, state_pool = states
    n_p = nb_p * t_p
    past_len = page_table.shape[1] * PAGE_SIZE

    w_main = jnp.concatenate([w_in[:, :C_FF_ORIG], w_in[:, C_FF_ORIG + H_ATT:]], axis=1).astype(BF16)
    wf_t = jnp.pad(w_in[:, C_FF_ORIG:C_FF_ORIG + H_ATT].T, ((0, SUBLANES - H_ATT), (0, 0))).astype(BF16)
    bf = jnp.pad(b_forget, (0, SUBLANES - H_ATT)).reshape(SUBLANES, 1)

    h, lf_t = _norm_forget(x, g_mix, wf_t, bf)
    proj = _matmul(h, w_main, 1024)

    out_a_p, conv_p = _conv_prompt(proj, conv_w, nb_p, t_p)
    pool_w_b = pool_w.astype(BF16)
    pool_scale2 = pool_scale.reshape(1, W_MIX)
    out_c_p, pool_p = _pool_prompt(proj, pool_w_b, pool_scale2, nb_p, t_p)
    f_cum = _cumsum_prompt(lf_t, nb_p, t_p)
    out_b_p = _attn_prompt(proj, nb_p, t_p, C_FQ, C_FK, C_FV, f_cum=f_cum)
    out_d_p = _attn_prompt(proj, nb_p, t_p, C_DQ, C_DK, C_DV, diff_lambda=diff_lambda,
                           diff_subln=diff_subln, lam_init=lam_init)

    proj_s = proj[n_p:].reshape(nb_s, t_s, N_PROJ)
    tmaj = lambda c: proj_s[:, :, c:c + W_MIX].transpose(1, 0, 2)
    out_a_s, conv_s = _conv_sample(tmaj(C_AX), tmaj(C_AB), tmaj(C_AC), state_conv.transpose(1, 0, 2), conv_w)
    out_c_s, pool_s = _pool_sample(tmaj(C_CX), state_pool.transpose(1, 0, 2), pool_w_b, pool_scale2, past_len)
    lf_new = lf_t[:, n_p:].reshape(SUBLANES, nb_s, t_s).transpose(1, 0, 2)
    out_b_s = _attn_sample(proj_s, cache_fox, page_table, layer, C_FQ, C_FK, C_FV,
                           lf_new=lf_new, cache_lf=cache_lf_t)
    out_d_s = _attn_sample(proj_s, cache_diff, page_table, layer, C_DQ, C_DK, C_DV,
                           diff_lambda=diff_lambda, diff_subln=diff_subln, lam_init=lam_init)

    def cat(p, s):
        return jnp.concatenate([p, s.reshape(nb_s * t_s, W_MIX).astype(BF16)], axis=0)

    branches = [cat(out_a_p, out_a_s.transpose(1, 0, 2)), cat(out_b_p, out_b_s),
                cat(out_c_p, out_c_s.transpose(1, 0, 2)), cat(out_d_p, out_d_s)]
    merged = _merge(h, branches, w_gate.astype(BF16), w_branch.astype(BF16),
                    b_gate.reshape(1, N_BRANCH * D_MODEL), 512)
    x = _matmul(merged, w_o.astype(BF16), 1024, res=x)

    lf_rows = lf_t[:H_ATT].T
    kv = lambda rows, c, nb, t: proj[rows, c:c + 2 * W_MIX].reshape(nb, t, 2, H_ATT, DH)
    rp, rs = slice(0, n_p), slice(n_p, None)
    st_p = (kv(rp, C_FK, nb_p, t_p), lf_rows[rp].reshape(nb_p, t_p, H_ATT), kv(rp, C_DK, nb_p, t_p),
            conv_p, pool_p)
    st_s = (kv(rs, C_FK, nb_s, t_s), lf_rows[rs].reshape(nb_s, t_s, H_ATT), kv(rs, C_DK, nb_s, t_s),
            conv_s.transpose(1, 0, 2), pool_s.transpose(1, 0, 2))
    return x, st_p, st_s


def kernel(x_prompt, x_sample, cache_fox_kv, cache_fox_logf, cache_diff_kv, state_conv, state_pool,
           page_table, norm_mix, w_in, b_forget, conv_w, pool_w, pool_scale, diff_lambda, diff_subln,
           w_branch, w_gate, b_gate, w_o, norm_ffn, ffn_w_in, ffn_w_out, moe_router, moe_w_in,
           moe_w_out, norm_final):
    nb_p, t_p, d = x_prompt.shape
    nb_s, t_s, _ = x_sample.shape
    n_p = nb_p * t_p
    depth, n_phys = cache_fox_kv.shape[:2]
    assert d == D_MODEL and depth == DEPTH
    assert (n_p + nb_s * t_s) % TM == 0 and t_p % T_SEQ == 0 and t_p % TQ == 0
    assert page_table.shape[1] % PAGES_PER_STEP == 0

    x = jnp.concatenate([x_prompt.reshape(n_p, d), x_sample.reshape(nb_s * t_s, d)], axis=0)
    caches = (cache_fox_kv.reshape(depth, n_phys, PAGE_SIZE, 2 * H_ATT * DH),
              cache_fox_logf.transpose(0, 1, 3, 2),
              cache_diff_kv.reshape(depth, n_phys, PAGE_SIZE, 2 * H_ATT * DH))
    st_p, st_s = [], []
    for l in range(depth):
        lam_init = 0.8 - 0.6 * math.exp(-0.3 * l)
        lp = (norm_mix[l], w_in[l], b_forget[l], conv_w[l], pool_w[l], pool_scale[l], diff_lambda[l],
              diff_subln[l], w_branch[l], w_gate[l], b_gate[l], w_o[l])
        x, sp, ss = _token_mixers(x, l, lam_init, nb_p, t_p, nb_s, t_s, caches,
                                  (state_conv[l], state_pool[l]), page_table, lp)
        st_p.append(sp)
        st_s.append(ss)
        j = l // 2
        if l % 2 == 0:
            hf = _norm_plain(x, norm_ffn[l], BF16)
            act = _swiglu_in(hf, ffn_w_in[j].astype(BF16), 512)
            x = _matmul(act, ffn_w_out[j].astype(BF16), 512, res=x)
        else:
            x = _moe(x, norm_ffn[l], moe_router[j], moe_w_in[j].astype(BF16), moe_w_out[j].astype(BF16))
    y = _norm_plain(x, norm_final, F32)
    outs = [y[:n_p].reshape(nb_p, t_p, d), y[n_p:].reshape(nb_s, t_s, d)]
    for sts in (st_p, st_s):
        for k in range(5):
            outs.append(jnp.stack([s[k] for s in sts]))
    return tuple(outs)
```

```python
import functools
import math

import jax
import jax.numpy as jnp
from jax import lax
from jax.experimental import pallas as pl
from jax.experimental.pallas import tpu as pltpu

F32 = jnp.float32
BF16 = jnp.bfloat16

D_MODEL = 2048
DEPTH = 2
PAGE_SIZE = 128
N_BRANCH = 4
W_MIX = D_MODEL // 4
CONV_W = 3
H_ATT = 4
DH = W_MIX // H_ATT
DQK_D = DH // 2
FOX_SCALE = DH ** -0.5
DIFF_SCALE = DQK_D ** -0.5
POOL_WINDOWS = (2, 4, 8, 16)
GC = W_MIX // len(POOL_WINDOWS)
POOL_BUF = max(POOL_WINDOWS) - 1
D_FF = 11 * D_MODEL // 4
N_EXP = 8
TOP_K = 2
D_EXP = 7 * D_MODEL // 2
EPS = 1e-6
NEG_INF = -1e30
LOG2E = math.log2(math.e)
C_AX, C_AB, C_AC, C_FQ, C_FK, C_FV, C_CX, C_DQ, C_DK, C_DV = (i * W_MIX for i in range(10))
N_PROJ = 10 * W_MIX
C_FF_ORIG = 6 * W_MIX

LANES = 128
SUBLANES = 8
VMEM_LIMIT = 56 * 1024 * 1024

TM = 640
T_SEQ = 512
TQ = 512
PAGES_PER_STEP = 8
TM_E = 256
TN_E = 1024


def _cparams(*sem):
    return pltpu.CompilerParams(dimension_semantics=sem, vmem_limit_bytes=VMEM_LIMIT)


def _nt_dot(a, b, **kw):
    return lax.dot_general(a, b, (((1,), (1,)), ((), ())), preferred_element_type=F32, **kw)


def _rms(x, g):
    return x * lax.rsqrt(jnp.mean(x * x, axis=-1, keepdims=True) + EPS) * g


def _log_sigmoid(z):
    return jnp.minimum(z, 0.0) - jnp.log1p(jnp.exp(-jnp.abs(z)))


def _norm_plain_body(x_ref, g_ref, h_ref):
    h_ref[...] = _rms(x_ref[...], g_ref[...]).astype(h_ref.dtype)


def _norm_plain(x, g, out_dtype):
    n, d = x.shape
    return pl.pallas_call(
        _norm_plain_body,
        out_shape=jax.ShapeDtypeStruct((n, d), out_dtype),
        grid=(n // TM,),
        in_specs=[pl.BlockSpec((TM, d), lambda i: (i, 0)),
                  pl.BlockSpec((1, d), lambda i: (0, 0))],
        out_specs=pl.BlockSpec((TM, d), lambda i: (i, 0)),
        compiler_params=_cparams("parallel"),
        name="norm_plain",
    )(x, g.reshape(1, d))


def _norm_forget_body(x_ref, g_ref, wf_ref, bf_ref, h_ref, lf_ref):
    hb = _rms(x_ref[...], g_ref[...]).astype(BF16)
    h_ref[...] = hb
    lf_ref[...] = _log_sigmoid(_nt_dot(wf_ref[...], hb) + bf_ref[...])


def _norm_forget(x, g, wf_t, bf):
    n, d = x.shape
    return pl.pallas_call(
        _norm_forget_body,
        out_shape=(jax.ShapeDtypeStruct((n, d), BF16), jax.ShapeDtypeStruct((SUBLANES, n), F32)),
        grid=(n // TM,),
        in_specs=[pl.BlockSpec((TM, d), lambda i: (i, 0)),
                  pl.BlockSpec((1, d), lambda i: (0, 0)),
                  pl.BlockSpec((SUBLANES, d), lambda i: (0, 0)),
                  pl.BlockSpec((SUBLANES, 1), lambda i: (0, 0))],
        out_specs=(pl.BlockSpec((TM, d), lambda i: (i, 0)),
                   pl.BlockSpec((SUBLANES, TM), lambda i: (0, i))),
        compiler_params=_cparams("parallel"),
        name="norm_forget",
    )(x, g.reshape(1, d), wf_t, bf)


def _norm_route_body(x_ref, g_ref, wr_ref, h_ref, ids_ref, wts_ref):
    h = _rms(x_ref[...], g_ref[...])
    h_ref[...] = h.astype(BF16)
    logits = jnp.dot(h, wr_ref[...], preferred_element_type=F32, precision=lax.Precision.HIGHEST)
    lane = lax.broadcasted_iota(jnp.int32, logits.shape, 1)
    logits = jnp.where(lane < N_EXP, logits, -jnp.inf)
    m1 = jnp.max(logits, axis=-1, keepdims=True)
    i1 = jnp.min(jnp.where(logits == m1, lane, LANES), axis=-1, keepdims=True)
    rest = jnp.where(lane == i1, -jnp.inf, logits)
    m2 = jnp.max(rest, axis=-1, keepdims=True)
    i2 = jnp.min(jnp.where(rest == m2, lane, LANES), axis=-1, keepdims=True)
    e2 = jnp.exp(m2 - m1)
    w1 = 1.0 / (1.0 + e2)
    w2 = e2 / (1.0 + e2)
    ids_ref[...] = jnp.where(lane == 0, i1, i2)
    wts_ref[...] = jnp.where(lane == 0, w1, w2)


def _norm_route(x, g, router_pad):
    n, d = x.shape
    return pl.pallas_call(
        _norm_route_body,
        out_shape=(jax.ShapeDtypeStruct((n, d), BF16),
                   jax.ShapeDtypeStruct((n, LANES), jnp.int32),
                   jax.ShapeDtypeStruct((n, LANES), F32)),
        grid=(n // TM,),
        in_specs=[pl.BlockSpec((TM, d), lambda i: (i, 0)),
                  pl.BlockSpec((1, d), lambda i: (0, 0)),
                  pl.BlockSpec((d, LANES), lambda i: (0, 0))],
        out_specs=(pl.BlockSpec((TM, d), lambda i: (i, 0)),
                   pl.BlockSpec((TM, LANES), lambda i: (i, 0)),
                   pl.BlockSpec((TM, LANES), lambda i: (i, 0))),
        compiler_params=_cparams("parallel"),
        name="norm_route",
    )(x, g.reshape(1, d), router_pad)


def _cast_when(first, pairs):
    @pl.when(first)
    def _():
        for src_ref, dst_ref in pairs:
            dst_ref[...] = src_ref[...].reshape(dst_ref.shape).astype(BF16)


def _mm_body(x_ref, w_ref, *rest, has_res):
    r_ref = rest[0] if has_res else None
    o_ref, wb_ref = rest[-2:]
    _cast_when(pl.program_id(1) == 0, [(w_ref, wb_ref)])
    acc = jnp.dot(x_ref[...], wb_ref[...], preferred_element_type=F32)
    if has_res:
        acc = r_ref[...] + acc
    o_ref[...] = acc.astype(o_ref.dtype)


def _matmul(x, w, tn, res=None, out_dtype=F32, wl=0):
    m, k = x.shape
    n = w.shape[2]
    in_specs = [pl.BlockSpec((TM, k), lambda j, i: (i, 0)),
                pl.BlockSpec((None, k, tn), lambda j, i: (wl, 0, j))]
    args = [x, w]
    if res is not None:
        in_specs.append(pl.BlockSpec((TM, tn), lambda j, i: (i, j)))
        args.append(res)
    return pl.pallas_call(
        functools.partial(_mm_body, has_res=res is not None),
        out_shape=jax.ShapeDtypeStruct((m, n), out_dtype),
        grid=(n // tn, m // TM),
        in_specs=in_specs,
        out_specs=pl.BlockSpec((TM, tn), lambda j, i: (i, j)),
        scratch_shapes=[pltpu.VMEM((k, tn), BF16)],
        compiler_params=_cparams("parallel", "arbitrary"),
        name="matmul",
    )(*args)


def _swiglu_in_body(x_ref, wg_ref, wu_ref, o_ref, wgb_ref, wub_ref):
    _cast_when(pl.program_id(1) == 0, [(wg_ref, wgb_ref), (wu_ref, wub_ref)])
    x = x_ref[...]
    g = jnp.dot(x, wgb_ref[...], preferred_element_type=F32)
    u = jnp.dot(x, wub_ref[...], preferred_element_type=F32)
    o_ref[...] = (g * jax.nn.sigmoid(g) * u).astype(o_ref.dtype)


def _swiglu_in(x, w_in, tn, wl=0):
    m, k = x.shape
    f = w_in.shape[2] // 2
    nj = f // tn
    return pl.pallas_call(
        _swiglu_in_body,
        out_shape=jax.ShapeDtypeStruct((m, f), BF16),
        grid=(nj, m // TM),
        in_specs=[pl.BlockSpec((TM, k), lambda j, i: (i, 0)),
                  pl.BlockSpec((None, k, tn), lambda j, i: (wl, 0, j)),
                  pl.BlockSpec((None, k, tn), lambda j, i: (wl, 0, nj + j))],
        out_specs=pl.BlockSpec((TM, tn), lambda j, i: (i, j)),
        scratch_shapes=[pltpu.VMEM((k, tn), BF16), pltpu.VMEM((k, tn), BF16)],
        compiler_params=_cparams("parallel", "arbitrary"),
        name="swiglu_in",
    )(x, w_in, w_in)


def _merge_body(*refs):
    h_ref = refs[0]
    br_refs = refs[1:1 + N_BRANCH]
    wg_refs = refs[1 + N_BRANCH:1 + 2 * N_BRANCH]
    wb_refs = refs[1 + 2 * N_BRANCH:1 + 3 * N_BRANCH]
    bg_refs = refs[1 + 3 * N_BRANCH:1 + 4 * N_BRANCH]
    o_ref = refs[1 + 4 * N_BRANCH]
    wgb_refs = refs[2 + 4 * N_BRANCH:2 + 5 * N_BRANCH]
    wbb_refs = refs[2 + 5 * N_BRANCH:2 + 6 * N_BRANCH]
    _cast_when(pl.program_id(1) == 0, list(zip(wg_refs + wb_refs, wgb_refs + wbb_refs)))
    h = h_ref[...]
    acc = None
    for n in range(N_BRANCH):
        gate = jax.nn.sigmoid(jnp.dot(h, wgb_refs[n][...], preferred_element_type=F32) + bg_refs[n][...])
        o = jnp.dot(br_refs[n][...], wbb_refs[n][...], preferred_element_type=F32)
        acc = gate * o if acc is None else acc + gate * o
    o_ref[...] = acc.astype(o_ref.dtype)


def _merge(h, branches, w_gate, w_branch, b_gate, tn, wl):
    m, d = h.shape
    nj = d // tn
    tok = lambda j, i: (i, 0)
    in_specs = [pl.BlockSpec((TM, d), tok)]
    in_specs += [pl.BlockSpec((TM, W_MIX), tok) for _ in range(N_BRANCH)]
    in_specs += [pl.BlockSpec((None, d, tn), functools.partial(lambda n, j, i: (wl, 0, n * nj + j), n))
                 for n in range(N_BRANCH)]
    in_specs += [pl.BlockSpec((None, None, W_MIX, tn), functools.partial(lambda n, j, i: (wl, n, 0, j), n))
                 for n in range(N_BRANCH)]
    in_specs += [pl.BlockSpec((1, tn), functools.partial(lambda n, j, i: (0, n * nj + j), n))
                 for n in range(N_BRANCH)]
    return pl.pallas_call(
        _merge_body,
        out_shape=jax.ShapeDtypeStruct((m, d), BF16),
        grid=(nj, m // TM),
        in_specs=in_specs,
        out_specs=pl.BlockSpec((TM, tn), lambda j, i: (i, j)),
        scratch_shapes=([pltpu.VMEM((d, tn), BF16)] * N_BRANCH + [pltpu.VMEM((W_MIX, tn), BF16)] * N_BRANCH),
        compiler_params=_cparams("parallel", "arbitrary"),
        name="merge",
    )(h, *branches, *([w_gate] * N_BRANCH), *([w_branch] * N_BRANCH), *([b_gate] * N_BRANCH))


def _conv_prompt_body(ax_ref, ab_ref, ac_ref, w_ref, o_ref, st_ref, ext_ref, *, tt):
    i = pl.program_id(1)

    @pl.when(i == 0)
    def _():
        ext_ref[0:SUBLANES, :] = jnp.zeros((SUBLANES, W_MIX), F32)

    u = ac_ref[...] * ax_ref[...]
    ext_ref[SUBLANES:, :] = u
    w = w_ref[...]
    conv = (w[0:1] * ext_ref[SUBLANES - 2:SUBLANES - 2 + tt, :]
            + w[1:2] * ext_ref[SUBLANES - 1:SUBLANES - 1 + tt, :] + w[2:3] * u)
    o_ref[...] = (ab_ref[...] * conv).astype(o_ref.dtype)
    ext_ref[0:SUBLANES, :] = ext_ref[tt:tt + SUBLANES, :]

    @pl.when(i == pl.num_programs(1) - 1)
    def _():
        st_ref[0] = ext_ref[tt + SUBLANES - (CONV_W - 1):tt + SUBLANES, :]


def _conv_prompt(proj, conv_w, nb, t):
    tt = T_SEQ
    nt = t // tt
    cb = lambda c: (lambda b, i: (b * nt + i, c // W_MIX))
    return pl.pallas_call(
        functools.partial(_conv_prompt_body, tt=tt),
        out_shape=(jax.ShapeDtypeStruct((nb * t, W_MIX), BF16),
                   jax.ShapeDtypeStruct((nb, CONV_W - 1, W_MIX), F32)),
        grid=(nb, nt),
        in_specs=[pl.BlockSpec((tt, W_MIX), cb(C_AX)),
                  pl.BlockSpec((tt, W_MIX), cb(C_AB)),
                  pl.BlockSpec((tt, W_MIX), cb(C_AC)),
                  pl.BlockSpec((CONV_W, W_MIX), lambda b, i: (0, 0))],
        out_specs=(pl.BlockSpec((tt, W_MIX), lambda b, i: (b * nt + i, 0)),
                   pl.BlockSpec((1, CONV_W - 1, W_MIX), lambda b, i: (b, 0, 0))),
        scratch_shapes=[pltpu.VMEM((tt + SUBLANES, W_MIX), F32)],
        compiler_params=_cparams("parallel", "arbitrary"),
        name="conv_prompt",
    )(proj, proj, proj, conv_w)


def _conv_sample_body(ax_ref, ab_ref, ac_ref, st_ref, w_ref, o_ref, nst_ref, *, t):
    w = w_ref[...]
    u = [ac_ref[j] * ax_ref[j] for j in range(t)]
    ext = [st_ref[j] for j in range(CONV_W - 1)] + u
    for j in range(t):
        conv = w[0:1] * ext[j]
        for c in range(1, CONV_W):
            conv = conv + w[c:c + 1] * ext[j + c]
        o_ref[j] = ab_ref[j] * conv
    for j in range(CONV_W - 1):
        nst_ref[j] = ext[t + j]


def _conv_sample(ax, ab, ac, state, conv_w):
    t, nb, _ = ax.shape
    return pl.pallas_call(
        functools.partial(_conv_sample_body, t=t),
        out_shape=(jax.ShapeDtypeStruct((t, nb, W_MIX), F32),
                   jax.ShapeDtypeStruct((CONV_W - 1, nb, W_MIX), F32)),
        name="conv_sample",
    )(ax, ab, ac, state, conv_w)


def _pool_prompt_body(c_ref, pw_ref, ps_ref, o_ref, st_ref, ext_ref, *, tt):
    i = pl.program_id(1)
    hist = 2 * SUBLANES

    @pl.when(i == 0)
    def _():
        ext_ref[0:hist, :] = jnp.zeros((hist, W_MIX), F32)

    c = c_ref[...]
    ext_ref[hist:, :] = c
    pos = i * tt + lax.broadcasted_iota(jnp.int32, (tt, 1), 0)
    for g, win in enumerate(POOL_WINDOWS):
        cols = slice(g * GC, (g + 1) * GC)
        cg = c[:, cols]
        s = cg
        for j in range(1, win):
            s = s + ext_ref[hist - j:hist - j + tt, cols]
        cnt = jnp.minimum(win, pos + 1).astype(F32)
        pooled = s / cnt - cg
        og = jnp.dot(pooled.astype(BF16), pw_ref[g], preferred_element_type=F32)
        o_ref[:, cols] = (og * ps_ref[:, cols]).astype(o_ref.dtype)
    ext_ref[0:hist, :] = ext_ref[tt:tt + hist, :]

    @pl.when(i == pl.num_programs(1) - 1)
    def _():
        st_ref[0] = ext_ref[tt + hist - POOL_BUF:tt + hist, :]


def _pool_prompt(proj, pool_w, pool_scale, nb, t):
    tt = T_SEQ
    nt = t // tt
    return pl.pallas_call(
        functools.partial(_pool_prompt_body, tt=tt),
        out_shape=(jax.ShapeDtypeStruct((nb * t, W_MIX), BF16),
                   jax.ShapeDtypeStruct((nb, POOL_BUF, W_MIX), F32)),
        grid=(nb, nt),
        in_specs=[pl.BlockSpec((tt, W_MIX), lambda b, i: (b * nt + i, C_CX // W_MIX)),
                  pl.BlockSpec((len(POOL_WINDOWS), GC, GC), lambda b, i: (0, 0, 0)),
                  pl.BlockSpec((1, W_MIX), lambda b, i: (0, 0))],
        out_specs=(pl.BlockSpec((tt, W_MIX), lambda b, i: (b * nt + i, 0)),
                   pl.BlockSpec((1, POOL_BUF, W_MIX), lambda b, i: (b, 0, 0))),
        scratch_shapes=[pltpu.VMEM((tt + 2 * SUBLANES, W_MIX), F32)],
        compiler_params=_cparams("parallel", "arbitrary"),
        name="pool_prompt",
    )(proj, pool_w, pool_scale)


def _pool_sample_body(c_ref, st_ref, pw_ref, ps_ref, o_ref, nst_ref, *, t, past_len):
    ext = [st_ref[j] for j in range(POOL_BUF)] + [c_ref[j] for j in range(t)]
    for g, win in enumerate(POOL_WINDOWS):
        cols = slice(g * GC, (g + 1) * GC)
        rows = []
        for j in range(t):
            s = ext[POOL_BUF + j][:, cols]
            for k in range(1, win):
                s = s + ext[POOL_BUF + j - k][:, cols]
            cnt = float(min(win, past_len + j + 1))
            rows.append(s / cnt - ext[POOL_BUF + j][:, cols])
        pooled = jnp.concatenate(rows, axis=0)
        og = jnp.dot(pooled.astype(BF16), pw_ref[g], preferred_element_type=F32) * ps_ref[:, cols]
        nb = og.shape[0] // t
        for j in range(t):
            o_ref[j, :, cols] = og[j * nb:(j + 1) * nb]
    for j in range(POOL_BUF):
        nst_ref[j] = ext[t + j]


def _pool_sample(c, state, pool_w, pool_scale, past_len):
    t, nb, _ = c.shape
    return pl.pallas_call(
        functools.partial(_pool_sample_body, t=t, past_len=past_len),
        out_shape=(jax.ShapeDtypeStruct((t, nb, W_MIX), F32),
                   jax.ShapeDtypeStruct((POOL_BUF, nb, W_MIX), F32)),
        name="pool_sample",
    )(c, state, pool_w, pool_scale)


def _upper_ones(n):
    r = lax.broadcasted_iota(jnp.int32, (n, n), 0)
    c = lax.broadcasted_iota(jnp.int32, (n, n), 1)
    return (r <= c).astype(F32)


def _cumsum_body(lf_ref, o_ref, carry_ref, *, tc):
    @pl.when(pl.program_id(1) == 0)
    def _():
        carry_ref[...] = jnp.zeros_like(carry_ref)

    f = jnp.dot(lf_ref[...], _upper_ones(tc), preferred_element_type=F32,
                precision=lax.Precision.HIGHEST) + carry_ref[...]
    o_ref[...] = f * LOG2E
    carry_ref[...] = f[:, tc - 1:tc]


def _cumsum_prompt(lf_t, nb, t):
    tc = T_SEQ
    nt = t // tc
    return pl.pallas_call(
        functools.partial(_cumsum_body, tc=tc),
        out_shape=jax.ShapeDtypeStruct((SUBLANES, nb * t), F32),
        grid=(nb, nt),
        in_specs=[pl.BlockSpec((SUBLANES, tc), lambda b, i: (0, b * nt + i))],
        out_specs=pl.BlockSpec((SUBLANES, tc), lambda b, i: (0, b * nt + i)),
        scratch_shapes=[pltpu.VMEM((SUBLANES, 1), F32)],
        compiler_params=_cparams("parallel", "arbitrary"),
        name="cumsum_prompt",
    )(lf_t)


def _diff_lambda(dl_ref, lam_init):
    dl = dl_ref[...]
    a = jnp.sum(dl[0:1] * dl[1:2], axis=-1, keepdims=True)
    b = jnp.sum(dl[2:3] * dl[3:4], axis=-1, keepdims=True)
    return jnp.exp(a) - jnp.exp(b) + lam_init


def _half_masks(shape):
    lane = lax.broadcasted_iota(jnp.int32, shape, len(shape) - 1)
    return lane < DQK_D, lane >= DQK_D


def _softmax_step(s, v, m_ref, l_ref, acc_ref, idx):
    m_prev = m_ref[idx]
    m_new = jnp.maximum(m_prev, jnp.max(s, axis=-1, keepdims=True))
    alpha = jnp.exp2(m_prev - m_new)
    p = jnp.exp2(s - m_new)
    l_ref[idx] = alpha * l_ref[idx] + jnp.sum(p, axis=-1, keepdims=True)
    acc_ref[idx] = alpha * acc_ref[idx] + jnp.dot(p.astype(BF16), v, preferred_element_type=F32)
    m_ref[idx] = m_new


def _attn_prompt_body(*refs, diff, lam_init, tq):
    if diff:
        q_ref, k_ref, v_ref, dl_ref, sub_ref, o_ref, m_ref, l_ref, acc_ref = refs
    else:
        q_ref, k_ref, v_ref, f_ref, o_ref, m_ref, l_ref, acc_ref = refs
    qi = pl.program_id(1)
    ki = pl.program_id(2)
    n_soft = 2 if diff else 1
    scale = (DIFF_SCALE if diff else FOX_SCALE) * LOG2E

    @pl.when(ki == 0)
    def _():
        m_ref[...] = jnp.full(m_ref.shape, NEG_INF, F32)
        l_ref[...] = jnp.zeros(l_ref.shape, F32)
        acc_ref[...] = jnp.zeros(acc_ref.shape, F32)

    def block(masked):
        if masked:
            causal = (lax.broadcasted_iota(jnp.int32, (tq, tq), 1)
                      <= lax.broadcasted_iota(jnp.int32, (tq, tq), 0))
        for h in range(H_ATT):
            cs = slice(h * DH, (h + 1) * DH)
            q = q_ref[:, cs] * scale
            k = k_ref[:, cs].astype(BF16)
            v = v_ref[:, cs].astype(BF16)
            if diff:
                lo, hi = _half_masks(q.shape)
                qs = (jnp.where(lo, q, 0.0), jnp.where(hi, q, 0.0))
            else:
                qs = (q,)
            for n in range(n_soft):
                s = _nt_dot(qs[n].astype(BF16), k)
                if not diff:
                    s = s - f_ref[h:h + 1, :]
                if masked:
                    s = jnp.where(causal, s, NEG_INF)
                _softmax_step(s, v, m_ref, l_ref, acc_ref, n * H_ATT + h)

    @pl.when(ki < qi)
    def _():
        block(False)

    @pl.when(ki == qi)
    def _():
        block(True)
        if diff:
            lam = _diff_lambda(dl_ref, lam_init)
        for h in range(H_ATT):
            if diff:
                o = acc_ref[h] / l_ref[h] - lam * (acc_ref[H_ATT + h] / l_ref[H_ATT + h])
                o = _rms(o, sub_ref[...]) * (1.0 - lam_init)
            else:
                o = acc_ref[h] / l_ref[h]
            o_ref[:, h * DH:(h + 1) * DH] = o.astype(o_ref.dtype)


def _attn_prompt(proj, nb, t, c_q, c_k, c_v, *, f_cum=None, diff_lambda=None, diff_subln=None,
                 lam_init=0.0):
    diff = f_cum is None
    tq = TQ
    nt = t // tq
    wide = H_ATT * DH
    n_state = (2 if diff else 1) * H_ATT
    in_specs = [pl.BlockSpec((tq, wide), lambda b, qi, ki: (b * nt + qi, c_q // wide)),
                pl.BlockSpec((tq, wide), lambda b, qi, ki: (b * nt + jnp.minimum(ki, qi), c_k // wide)),
                pl.BlockSpec((tq, wide), lambda b, qi, ki: (b * nt + jnp.minimum(ki, qi), c_v // wide))]
    args = [proj, proj, proj]
    if diff:
        in_specs += [pl.BlockSpec((4, DQK_D), lambda b, qi, ki: (0, 0)),
                     pl.BlockSpec((1, DH), lambda b, qi, ki: (0, 0))]
        args += [diff_lambda, diff_subln.reshape(1, DH)]
    else:
        in_specs += [pl.BlockSpec((SUBLANES, tq), lambda b, qi, ki: (0, b * nt + jnp.minimum(ki, qi)))]
        args += [f_cum]
    return pl.pallas_call(
        functools.partial(_attn_prompt_body, diff=diff, lam_init=lam_init, tq=tq),
        out_shape=jax.ShapeDtypeStruct((nb * t, wide), BF16),
        grid=(nb, nt, nt),
        in_specs=in_specs,
        out_specs=pl.BlockSpec((tq, wide), lambda b, qi, ki: (b * nt + qi, 0)),
        scratch_shapes=[pltpu.VMEM((n_state, tq, 1), F32), pltpu.VMEM((n_state, tq, 1), F32),
                        pltpu.VMEM((n_state, tq, DH), F32)],
        compiler_params=_cparams("parallel", "parallel", "arbitrary"),
        name="attn_prompt_diff" if diff else "attn_prompt_fox",
    )(*args)


def _attn_sample_body(*refs, diff, lam_init, t):
    np_ = PAGES_PER_STEP
    pt_ref, q_ref, kn_ref, vn_ref = refs[:4]
    if diff:
        dl_ref, sub_ref = refs[4:6]
        pages = refs[6:6 + np_]
        o_ref, m_ref, l_ref, acc_ref = refs[6 + np_:]
    else:
        lfn_ref = refs[4]
        pages = refs[5:5 + np_]
        lf_pages = refs[5 + np_:5 + 2 * np_]
        o_ref, m_ref, l_ref, acc_ref, carry_ref = refs[5 + 2 * np_:]
    del pt_ref
    step = pl.program_id(1)
    last = pl.num_programs(1) - 1
    r = 2 * t if diff else t
    scale = (DIFF_SCALE if diff else FOX_SCALE) * LOG2E
    row_stride = 2 * H_ATT

    @pl.when(step == 0)
    def _():
        m_ref[...] = jnp.full(m_ref.shape, NEG_INF, F32)
        l_ref[...] = jnp.zeros(l_ref.shape, F32)
        acc_ref[...] = jnp.zeros(acc_ref.shape, F32)
        if not diff:
            carry_ref[...] = jnp.zeros(carry_ref.shape, F32)

    q_all = q_ref[0] * scale

    def head_q(h):
        qh = q_all[:, h * DH:(h + 1) * DH]
        if diff:
            lo, hi = _half_masks(qh.shape)
            qh = jnp.concatenate([jnp.where(lo, qh, 0.0), jnp.where(hi, qh, 0.0)], axis=0)
        return qh

    def per_head(x, first, stride, rows):
        return jnp.concatenate(
            [jnp.broadcast_to(x[first + h * stride:first + h * stride + 1], (rows, x.shape[1]))
             for h in range(H_ATT)], axis=0)

    def cache_rows(first):
        return jnp.concatenate([pg[pl.ds(first, PAGE_SIZE, stride=row_stride), :] for pg in pages],
                               axis=0).astype(BF16)

    s = jnp.concatenate([_nt_dot(head_q(h).astype(BF16), cache_rows(h)) for h in range(H_ATT)],
                        axis=0)
    if not diff:
        lf = jnp.concatenate([lp[pl.ds(h, 1), :] for h in range(H_ATT) for lp in lf_pages], axis=0)
        n_lf = H_ATT * np_
        ra = lax.broadcasted_iota(jnp.int32, (n_lf, n_lf), 0)
        rb = lax.broadcasted_iota(jnp.int32, (n_lf, n_lf), 1)
        earlier = ((ra // np_ == rb // np_) & (rb < ra)).astype(F32)
        prev = jnp.dot(earlier, lf, preferred_element_type=F32, precision=lax.Precision.HIGHEST)
        rr = lax.broadcasted_iota(jnp.int32, (2 * PAGE_SIZE, PAGE_SIZE), 0)
        cc = lax.broadcasted_iota(jnp.int32, (2 * PAGE_SIZE, PAGE_SIZE), 1)
        tri_ones = ((rr <= cc) | (rr >= PAGE_SIZE)).astype(F32)
        f = jnp.dot(jnp.concatenate([lf, prev], axis=1), tri_ones, preferred_element_type=F32,
                    precision=lax.Precision.HIGHEST) + carry_ref[...]
        carry_ref[...] = per_head(f[:, PAGE_SIZE - 1:PAGE_SIZE], np_ - 1, np_, np_)
        f = f * LOG2E
        bias = jnp.concatenate(
            [jnp.broadcast_to(jnp.concatenate([f[h * np_ + i:h * np_ + i + 1] for i in range(np_)], axis=1),
                              (r, np_ * PAGE_SIZE)) for h in range(H_ATT)], axis=0)
        s = s - bias
    m_prev = m_ref[...]
    m_new = jnp.maximum(m_prev, jnp.max(s, axis=-1, keepdims=True))
    alpha = jnp.exp2(m_prev - m_new)
    p = jnp.exp2(s - m_new)
    l_ref[...] = alpha * l_ref[...] + jnp.sum(p, axis=-1, keepdims=True)
    pv = jnp.concatenate([jnp.dot(p[h * r:(h + 1) * r].astype(BF16), cache_rows(H_ATT + h),
                                  preferred_element_type=F32) for h in range(H_ATT)], axis=0)
    acc_ref[...] = alpha * acc_ref[...] + pv
    m_ref[...] = m_new

    @pl.when(step == last)
    def _():
        qrow = lax.broadcasted_iota(jnp.int32, (H_ATT * r, 1), 0) % t
        qh = [head_q(h) for h in range(H_ATT)]
        kn = kn_ref[0]
        vn = vn_ref[0]
        m = m_ref[...]
        l = l_ref[...]
        acc = acc_ref[...]
        if not diff:
            off = per_head(carry_ref[...], 0, np_, r)
            lfn = lfn_ref[0]
        for j in range(t):
            sj = jnp.concatenate(
                [jnp.sum(qh[h] * kn[j:j + 1, h * DH:(h + 1) * DH], axis=-1, keepdims=True)
                 for h in range(H_ATT)], axis=0)
            if not diff:
                off = off + jnp.concatenate(
                    [jnp.broadcast_to(lfn[h:h + 1, j:j + 1], (r, 1)) for h in range(H_ATT)], axis=0)
                sj = sj - off * LOG2E
            sj = jnp.where(qrow >= j, sj, NEG_INF)
            m_new = jnp.maximum(m, sj)
            alpha = jnp.exp2(m - m_new)
            pj = jnp.exp2(sj - m_new)
            l = alpha * l + pj
            vj = jnp.concatenate([jnp.broadcast_to(vn[j:j + 1, h * DH:(h + 1) * DH], (r, DH))
                                  for h in range(H_ATT)], axis=0)
            acc = alpha * acc + pj * vj
            m = m_new
        o_all = acc / l
        if diff:
            lam = _diff_lambda(dl_ref, lam_init)
        for h in range(H_ATT):
            o = o_all[h * r:(h + 1) * r]
            if diff:
                o = o[0:t] - lam * o[t:2 * t]
                o = _rms(o, sub_ref[...]) * (1.0 - lam_init)
            o_ref[0, :, h * DH:(h + 1) * DH] = o


def _attn_sample(proj_s, cache, page_table, layer, c_q, c_k, c_v, *, lf_new=None, cache_lf=None,
                 diff_lambda=None, diff_subln=None, lam_init=0.0):
    diff = cache_lf is None
    nb, t, _ = proj_s.shape
    n_pages = page_table.shape[1]
    np_ = PAGES_PER_STEP
    n_steps = n_pages // np_
    wide = H_ATT * DH
    r = 2 * t if diff else t

    def page_map(i):
        return lambda b, s, pt: (layer, pt[b * n_pages + s * np_ + i], 0, 0)

    in_specs = [pl.BlockSpec((1, t, wide), lambda b, s, pt: (b, 0, c_q // wide)),
                pl.BlockSpec((1, t, wide), lambda b, s, pt: (b, 0, c_k // wide)),
                pl.BlockSpec((1, t, wide), lambda b, s, pt: (b, 0, c_v // wide))]
    args = [proj_s, proj_s, proj_s]
    if diff:
        in_specs += [pl.BlockSpec((4, DQK_D), lambda b, s, pt: (0, 0)),
                     pl.BlockSpec((1, DH), lambda b, s, pt: (0, 0))]
        args += [diff_lambda, diff_subln.reshape(1, DH)]
    else:
        in_specs += [pl.BlockSpec((1, SUBLANES, t), lambda b, s, pt: (b, 0, 0))]
        args += [lf_new]
    in_specs += [pl.BlockSpec((None, None, PAGE_SIZE * 2 * H_ATT, DH), page_map(i)) for i in range(np_)]
    args += [cache] * np_
    scratch = [pltpu.VMEM((H_ATT * r, 1), F32), pltpu.VMEM((H_ATT * r, 1), F32),
               pltpu.VMEM((H_ATT * r, DH), F32)]
    if not diff:
        in_specs += [pl.BlockSpec((None, None, H_ATT, PAGE_SIZE), page_map(i)) for i in range(np_)]
        args += [cache_lf] * np_
        scratch += [pltpu.VMEM((H_ATT * np_, 1), F32)]
    return pl.pallas_call(
        functools.partial(_attn_sample_body, diff=diff, lam_init=lam_init, t=t),
        out_shape=jax.ShapeDtypeStruct((nb, t, wide), F32),
        grid_spec=pltpu.PrefetchScalarGridSpec(
            num_scalar_prefetch=1,
            grid=(nb, n_steps),
            in_specs=in_specs,
            out_specs=pl.BlockSpec((1, t, wide), lambda b, s, pt: (b, 0, 0)),
            scratch_shapes=scratch),
        compiler_params=_cparams("parallel", "arbitrary"),
        name="attn_sample_diff" if diff else "attn_sample_fox",
    )(page_table.reshape(-1), *args)


def _new_expert(te_ref):
    i = pl.program_id(1)
    return (i == 0) | (te_ref[i] != te_ref[jnp.maximum(i - 1, 0)])


def _moe_in_body(te_ref, nt_ref, x_ref, wg_ref, wu_ref, o_ref, wgb_ref, wub_ref):
    _cast_when(_new_expert(te_ref), [(wg_ref, wgb_ref), (wu_ref, wub_ref)])

    @pl.when(pl.program_id(1) < nt_ref[0])
    def _():
        x = x_ref[...]
        g = jnp.dot(x, wgb_ref[...], preferred_element_type=F32)
        u = jnp.dot(x, wub_ref[...], preferred_element_type=F32)
        o_ref[...] = (g * jax.nn.sigmoid(g) * u).astype(o_ref.dtype)

    @pl.when(pl.program_id(1) >= nt_ref[0])
    def _():
        o_ref[...] = jnp.zeros(o_ref.shape, o_ref.dtype)


def _moe_in(xs, w_in, tile_expert, n_tiles, wl):
    p, d = xs.shape
    f = w_in.shape[3] // 2
    nj = f // TN_E
    return pl.pallas_call(
        _moe_in_body,
        out_shape=jax.ShapeDtypeStruct((p, f), BF16),
        grid_spec=pltpu.PrefetchScalarGridSpec(
            num_scalar_prefetch=2,
            grid=(nj, p // TM_E),
            in_specs=[pl.BlockSpec((TM_E, d), lambda j, i, te, nt: (i, 0)),
                      pl.BlockSpec((None, None, d, TN_E), lambda j, i, te, nt: (wl, te[i], 0, j)),
                      pl.BlockSpec((None, None, d, TN_E), lambda j, i, te, nt: (wl, te[i], 0, nj + j))],
            out_specs=pl.BlockSpec((TM_E, TN_E), lambda j, i, te, nt: (i, j)),
            scratch_shapes=[pltpu.VMEM((d, TN_E), BF16), pltpu.VMEM((d, TN_E), BF16)]),
        compiler_params=_cparams("parallel", "arbitrary"),
        name="moe_in",
    )(tile_expert, n_tiles, xs, w_in, w_in)


def _moe_out_body(te_ref, nt_ref, a_ref, w_ref, ws_ref, o_ref, wb_ref):
    _cast_when(_new_expert(te_ref), [(w_ref, wb_ref)])

    @pl.when(pl.program_id(1) < nt_ref[0])
    def _():
        o_ref[...] = jnp.dot(a_ref[...], wb_ref[...], preferred_element_type=F32) * ws_ref[...]

    @pl.when(pl.program_id(1) >= nt_ref[0])
    def _():
        o_ref[...] = jnp.zeros(o_ref.shape, o_ref.dtype)


def _moe_out(act, w_out, w_slot, tile_expert, n_tiles, tn, wl):
    p, f = act.shape
    d = w_out.shape[3]
    return pl.pallas_call(
        _moe_out_body,
        out_shape=jax.ShapeDtypeStruct((p, d), F32),
        grid_spec=pltpu.PrefetchScalarGridSpec(
            num_scalar_prefetch=2,
            grid=(d // tn, p // TM_E),
            in_specs=[pl.BlockSpec((TM_E, f), lambda j, i, te, nt: (i, 0)),
                      pl.BlockSpec((None, None, f, tn), lambda j, i, te, nt: (wl, te[i], 0, j)),
                      pl.BlockSpec((TM_E, 1), lambda j, i, te, nt: (i, 0))],
            out_specs=pl.BlockSpec((TM_E, tn), lambda j, i, te, nt: (i, j)),
            scratch_shapes=[pltpu.VMEM((f, tn), BF16)]),
        compiler_params=_cparams("parallel", "arbitrary"),
        name="moe_out",
    )(tile_expert, n_tiles, act, w_out, w_slot)


def _combine_body(x_ref, g_ref, o_ref):
    o_ref[...] = x_ref[...] + g_ref[:, 0, :] + g_ref[:, 1, :]


def _combine(x, g):
    n, d = x.shape
    tm = TM // 2
    return pl.pallas_call(
        _combine_body,
        out_shape=jax.ShapeDtypeStruct((n, d), F32),
        grid=(n // tm,),
        in_specs=[pl.BlockSpec((tm, d), lambda i: (i, 0)),
                  pl.BlockSpec((tm, TOP_K, d), lambda i: (i, 0, 0))],
        out_specs=pl.BlockSpec((tm, d), lambda i: (i, 0)),
        compiler_params=_cparams("parallel"),
        name="moe_combine",
    )(x, g)


def _moe(x, g_norm, router, w_in, w_out, wl):
    n, d = x.shape
    router_pad = jnp.pad(router, ((0, 0), (0, LANES - N_EXP)))
    h, ids, wts = _norm_route(x, g_norm, router_pad)
    n_slots = n * TOP_K
    p_total = n_slots + N_EXP * TM_E
    flat_e = ids[:, :TOP_K].reshape(-1)
    onehot = (flat_e[:, None] == jnp.arange(N_EXP, dtype=jnp.int32)[None, :]).astype(jnp.int32)
    rank = jnp.sum((jnp.cumsum(onehot, axis=0) - onehot) * onehot, axis=1)
    counts = jnp.sum(onehot, axis=0)
    padded = (counts + TM_E - 1) // TM_E * TM_E
    ends = jnp.cumsum(padded)
    dest = (ends - padded)[flat_e] + rank
    src_tok = jnp.zeros((p_total,), jnp.int32).at[dest].set(jnp.arange(n_slots, dtype=jnp.int32) // TOP_K)
    w_slot = jnp.zeros((p_total,), F32).at[dest].set(wts[:, :TOP_K].reshape(-1))
    n_tiles = (ends[-1] // TM_E).astype(jnp.int32).reshape(1)
    tile_start = jnp.arange(p_total // TM_E, dtype=jnp.int32) * TM_E
    tile_expert = jnp.minimum(jnp.sum((tile_start[:, None] >= ends[None, :]).astype(jnp.int32), axis=1),
                              N_EXP - 1)
    last_valid = jnp.maximum(n_tiles[0] - 1, 0)
    tile_expert = jnp.where(tile_start // TM_E < n_tiles[0], tile_expert, tile_expert[last_valid])

    xs = jnp.take(h, src_tok, axis=0)
    act = _moe_in(xs, w_in, tile_expert, n_tiles, wl)
    ys = _moe_out(act, w_out, w_slot.reshape(p_total, 1), tile_expert, n_tiles, 512, wl)
    g = jnp.take(ys, dest.reshape(n, TOP_K), axis=0)
    return _combine(x, g)


def _token_mixers(x, layer, lam_init, nb_p, t_p, nb_s, t_s, caches, states, page_table, lp):
    (g_mix, w_in, b_forget, conv_w, pool_w, pool_scale, diff_lambda, diff_subln,
     w_branch, w_gate, b_gate, w_o) = lp
    cache_fox, cache_lf_t, cache_diff = caches
    state_conv, state_pool = states
    n_p = nb_p * t_p
    past_len = page_table.shape[1] * PAGE_SIZE

    w_main = jnp.concatenate([w_in[:, :C_FF_ORIG], w_in[:, C_FF_ORIG + H_ATT:]], axis=1).astype(BF16)
    wf_t = jnp.pad(w_in[:, C_FF_ORIG:C_FF_ORIG + H_ATT].T, ((0, SUBLANES - H_ATT), (0, 0))).astype(BF16)
    bf = jnp.pad(b_forget, (0, SUBLANES - H_ATT)).reshape(SUBLANES, 1)

    h, lf_t = _norm_forget(x, g_mix, wf_t, bf)
    proj = _matmul(h, w_main[None], 1024)

    out_a_p, conv_p = _conv_prompt(proj, conv_w, nb_p, t_p)
    pool_w_b = pool_w.astype(BF16)
    pool_scale2 = pool_scale.reshape(1, W_MIX)
    out_c_p, pool_p = _pool_prompt(proj, pool_w_b, pool_scale2, nb_p, t_p)
    f_cum = _cumsum_prompt(lf_t, nb_p, t_p)
    out_b_p = _attn_prompt(proj, nb_p, t_p, C_FQ, C_FK, C_FV, f_cum=f_cum)
    out_d_p = _attn_prompt(proj, nb_p, t_p, C_DQ, C_DK, C_DV, diff_lambda=diff_lambda,
                           diff_subln=diff_subln, lam_init=lam_init)

    proj_s = proj[n_p:].reshape(nb_s, t_s, N_PROJ)
    tmaj = lambda c: proj_s[:, :, c:c + W_MIX].transpose(1, 0, 2)
    out_a_s, conv_s = _conv_sample(tmaj(C_AX), tmaj(C_AB), tmaj(C_AC), state_conv.transpose(1, 0, 2), conv_w)
    out_c_s, pool_s = _pool_sample(tmaj(C_CX), state_pool.transpose(1, 0, 2), pool_w_b, pool_scale2, past_len)
    lf_new = lf_t[:, n_p:].reshape(SUBLANES, nb_s, t_s).transpose(1, 0, 2)
    out_b_s = _attn_sample(proj_s, cache_fox, page_table, layer, C_FQ, C_FK, C_FV,
                           lf_new=lf_new, cache_lf=cache_lf_t)
    out_d_s = _attn_sample(proj_s, cache_diff, page_table, layer, C_DQ, C_DK, C_DV,
                           diff_lambda=diff_lambda, diff_subln=diff_subln, lam_init=lam_init)

    def cat(p, s):
        return jnp.concatenate([p, s.reshape(nb_s * t_s, W_MIX).astype(BF16)], axis=0)

    branches = [cat(out_a_p, out_a_s.transpose(1, 0, 2)), cat(out_b_p, out_b_s),
                cat(out_c_p, out_c_s.transpose(1, 0, 2)), cat(out_d_p, out_d_s)]
    merged = _merge(h, branches, w_gate, w_branch, b_gate.reshape(1, N_BRANCH * D_MODEL), 256, layer)
    x = _matmul(merged, w_o, 1024, res=x, wl=layer)

    lf_rows = lf_t[:H_ATT].T
    kv = lambda rows, c, nb, t: proj[rows, c:c + 2 * W_MIX].reshape(nb, t, 2, H_ATT, DH)
    rp, rs = slice(0, n_p), slice(n_p, None)
    st_p = (kv(rp, C_FK, nb_p, t_p), lf_rows[rp].reshape(nb_p, t_p, H_ATT), kv(rp, C_DK, nb_p, t_p),
            conv_p, pool_p)
    st_s = (kv(rs, C_FK, nb_s, t_s), lf_rows[rs].reshape(nb_s, t_s, H_ATT), kv(rs, C_DK, nb_s, t_s),
            conv_s.transpose(1, 0, 2), pool_s.transpose(1, 0, 2))
    return x, st_p, st_s


def kernel(x_prompt, x_sample, cache_fox_kv, cache_fox_logf, cache_diff_kv, state_conv, state_pool,
           page_table, norm_mix, w_in, b_forget, conv_w, pool_w, pool_scale, diff_lambda, diff_subln,
           w_branch, w_gate, b_gate, w_o, norm_ffn, ffn_w_in, ffn_w_out, moe_router, moe_w_in,
           moe_w_out, norm_final):
    nb_p, t_p, d = x_prompt.shape
    nb_s, t_s, _ = x_sample.shape
    n_p = nb_p * t_p
    depth, n_phys = cache_fox_kv.shape[:2]
    assert d == D_MODEL and depth == DEPTH
    assert (n_p + nb_s * t_s) % TM == 0 and t_p % T_SEQ == 0 and t_p % TQ == 0
    assert page_table.shape[1] % PAGES_PER_STEP == 0

    x = jnp.concatenate([x_prompt.reshape(n_p, d), x_sample.reshape(nb_s * t_s, d)], axis=0)
    caches = (cache_fox_kv.reshape(depth, n_phys, PAGE_SIZE * 2 * H_ATT, DH),
              cache_fox_logf.transpose(0, 1, 3, 2),
              cache_diff_kv.reshape(depth, n_phys, PAGE_SIZE * 2 * H_ATT, DH))
    st_p, st_s = [], []
    for l in range(depth):
        lam_init = 0.8 - 0.6 * math.exp(-0.3 * l)
        lp = (norm_mix[l], w_in[l], b_forget[l], conv_w[l], pool_w[l], pool_scale[l], diff_lambda[l],
              diff_subln[l], w_branch, w_gate, b_gate[l], w_o)
        x, sp, ss = _token_mixers(x, l, lam_init, nb_p, t_p, nb_s, t_s, caches,
                                  (state_conv[l], state_pool[l]), page_table, lp)
        st_p.append(sp)
        st_s.append(ss)
        j = l // 2
        if l % 2 == 0:
            hf = _norm_plain(x, norm_ffn[l], BF16)
            act = _swiglu_in(hf, ffn_w_in, 512, wl=j)
            x = _matmul(act, ffn_w_out, 256, res=x, wl=j)
        else:
            x = _moe(x, norm_ffn[l], moe_router[j], moe_w_in, moe_w_out, j)
    y = _norm_plain(x, norm_final, F32)
    outs = [y[:n_p].reshape(nb_p, t_p, d), y[n_p:].reshape(nb_s, t_s, d)]
    for sts in (st_p, st_s):
        for k in range(5):
            outs.append(jnp.stack([s[k] for s in sts]))
    return tuple(outs)
```

```python
import functools
import math

import jax
import jax.numpy as jnp
from jax import lax
from jax.experimental import pallas as pl
from jax.experimental.pallas import tpu as pltpu

F32 = jnp.float32
BF16 = jnp.bfloat16

D_MODEL = 2048
DEPTH = 2
PAGE_SIZE = 128
N_BRANCH = 4
W_MIX = D_MODEL // 4
CONV_W = 3
H_ATT = 4
DH = W_MIX // H_ATT
DQK_D = DH // 2
FOX_SCALE = DH ** -0.5
DIFF_SCALE = DQK_D ** -0.5
POOL_WINDOWS = (2, 4, 8, 16)
GC = W_MIX // len(POOL_WINDOWS)
POOL_BUF = max(POOL_WINDOWS) - 1
D_FF = 11 * D_MODEL // 4
N_EXP = 8
TOP_K = 2
D_EXP = 7 * D_MODEL // 2
EPS = 1e-6
NEG_INF = -1e30
LOG2E = math.log2(math.e)
C_AX, C_AB, C_AC, C_FQ, C_FK, C_FV = (i * W_MIX for i in range(6))
C_CX, C_DQ, C_DK, C_DV = (i * W_MIX for i in range(4))
N_LO = 6 * W_MIX
N_HI = 4 * W_MIX

LANES = 128
SUBLANES = 8
VMEM_LIMIT = 56 * 1024 * 1024

TM = 640
T_SEQ = 512
TQ = 512
PAGES_PER_STEP = 8
TM_E = 256
TN_E = 1024


def _cparams(*sem):
    return pltpu.CompilerParams(dimension_semantics=sem, vmem_limit_bytes=VMEM_LIMIT)


def _nt_dot(a, b, **kw):
    return lax.dot_general(a, b, (((1,), (1,)), ((), ())), preferred_element_type=F32, **kw)


def _rms(x, g):
    return x * lax.rsqrt(jnp.mean(x * x, axis=-1, keepdims=True) + EPS) * g


def _log_sigmoid(z):
    return jnp.minimum(z, 0.0) - jnp.log1p(jnp.exp(-jnp.abs(z)))


def _norm_plain_body(x_ref, g_ref, h_ref):
    h_ref[...] = _rms(x_ref[...], g_ref[...]).astype(h_ref.dtype)


def _norm_plain(x, g, out_dtype):
    n, d = x.shape
    return pl.pallas_call(
        _norm_plain_body,
        out_shape=jax.ShapeDtypeStruct((n, d), out_dtype),
        grid=(n // TM,),
        in_specs=[pl.BlockSpec((TM, d), lambda i: (i, 0)),
                  pl.BlockSpec((1, d), lambda i: (0, 0))],
        out_specs=pl.BlockSpec((TM, d), lambda i: (i, 0)),
        compiler_params=_cparams("parallel"),
        name="norm_plain",
    )(x, g.reshape(1, d))


def _norm_forget_body(x_ref, g_ref, wf_ref, bf_ref, h_ref, lf_ref):
    hb = _rms(x_ref[...], g_ref[...]).astype(BF16)
    h_ref[...] = hb
    lf_ref[...] = _log_sigmoid(_nt_dot(wf_ref[...].astype(BF16), hb) + bf_ref[...])


def _norm_forget(x, g, wf_t, bf):
    n, d = x.shape
    return pl.pallas_call(
        _norm_forget_body,
        out_shape=(jax.ShapeDtypeStruct((n, d), BF16), jax.ShapeDtypeStruct((SUBLANES, n), F32)),
        grid=(n // TM,),
        in_specs=[pl.BlockSpec((TM, d), lambda i: (i, 0)),
                  pl.BlockSpec((1, d), lambda i: (0, 0)),
                  pl.BlockSpec((SUBLANES, d), lambda i: (0, 0)),
                  pl.BlockSpec((SUBLANES, 1), lambda i: (0, 0))],
        out_specs=(pl.BlockSpec((TM, d), lambda i: (i, 0)),
                   pl.BlockSpec((SUBLANES, TM), lambda i: (0, i))),
        compiler_params=_cparams("parallel"),
        name="norm_forget",
    )(x, g.reshape(1, d), wf_t, bf)


def _norm_route_body(x_ref, g_ref, wr_ref, h_ref, ids_ref, wts_ref, rank_ref, cnt_ref, carry_ref):
    @pl.when(pl.program_id(0) == 0)
    def _():
        carry_ref[...] = jnp.zeros(carry_ref.shape, F32)

    h = _rms(x_ref[...], g_ref[...])
    h_ref[...] = h.astype(BF16)
    logits = jnp.dot(h, wr_ref[...], preferred_element_type=F32, precision=lax.Precision.HIGHEST)
    lane = lax.broadcasted_iota(jnp.int32, logits.shape, 1)
    logits = jnp.where(lane < N_EXP, logits, -jnp.inf)
    m1 = jnp.max(logits, axis=-1, keepdims=True)
    i1 = jnp.min(jnp.where(logits == m1, lane, LANES), axis=-1, keepdims=True)
    rest = jnp.where(lane == i1, -jnp.inf, logits)
    m2 = jnp.max(rest, axis=-1, keepdims=True)
    i2 = jnp.min(jnp.where(rest == m2, lane, LANES), axis=-1, keepdims=True)
    e2 = jnp.exp(m2 - m1)
    w1 = 1.0 / (1.0 + e2)
    w2 = e2 / (1.0 + e2)
    ids_ref[...] = jnp.where(lane == 0, i1, i2)
    wts_ref[...] = jnp.where(lane == 0, w1, w2)
    chosen = (lane == i1) | (lane == i2)
    tm = chosen.shape[0]
    earlier = (lax.broadcasted_iota(jnp.int32, (tm, tm), 1)
               < lax.broadcasted_iota(jnp.int32, (tm, tm), 0))
    before = jnp.dot(jnp.where(earlier, 1.0, 0.0).astype(BF16), jnp.where(chosen, 1.0, 0.0).astype(BF16),
                     preferred_element_type=F32) + carry_ref[...]
    r1 = jnp.sum(jnp.where(lane == i1, before, 0.0), axis=-1, keepdims=True)
    r2 = jnp.sum(jnp.where(lane == i2, before, 0.0), axis=-1, keepdims=True)
    rank_ref[...] = jnp.where(lane == 0, r1, r2).astype(jnp.int32)
    total = carry_ref[...] + jnp.sum(jnp.where(chosen, 1.0, 0.0), axis=0, keepdims=True)
    carry_ref[...] = total
    cnt_ref[...] = jnp.broadcast_to(total, cnt_ref.shape)


def _norm_route(x, g, router_pad):
    n, d = x.shape
    return pl.pallas_call(
        _norm_route_body,
        out_shape=(jax.ShapeDtypeStruct((n, d), BF16),
                   jax.ShapeDtypeStruct((n, LANES), jnp.int32),
                   jax.ShapeDtypeStruct((n, LANES), F32),
                   jax.ShapeDtypeStruct((n, LANES), jnp.int32),
                   jax.ShapeDtypeStruct((SUBLANES, LANES), F32)),
        grid=(n // TM,),
        in_specs=[pl.BlockSpec((TM, d), lambda i: (i, 0)),
                  pl.BlockSpec((1, d), lambda i: (0, 0)),
                  pl.BlockSpec((d, LANES), lambda i: (0, 0))],
        out_specs=(pl.BlockSpec((TM, d), lambda i: (i, 0)),
                   pl.BlockSpec((TM, LANES), lambda i: (i, 0)),
                   pl.BlockSpec((TM, LANES), lambda i: (i, 0)),
                   pl.BlockSpec((TM, LANES), lambda i: (i, 0)),
                   pl.BlockSpec((SUBLANES, LANES), lambda i: (0, 0))),
        scratch_shapes=[pltpu.VMEM((1, LANES), F32)],
        compiler_params=_cparams("arbitrary"),
        name="norm_route",
    )(x, g.reshape(1, d), router_pad)


def _cast_when(first, pairs):
    @pl.when(first)
    def _():
        for src_ref, dst_ref in pairs:
            dst_ref[...] = src_ref[...].reshape(dst_ref.shape).astype(BF16)


def _mm_body(x_ref, w_ref, *rest, has_res):
    r_ref = rest[0] if has_res else None
    o_ref, wb_ref = rest[-2:]
    _cast_when(pl.program_id(1) == 0, [(w_ref, wb_ref)])
    acc = jnp.dot(x_ref[...], wb_ref[...], preferred_element_type=F32)
    if has_res:
        acc = r_ref[...] + acc
    o_ref[...] = acc.astype(o_ref.dtype)


def _matmul(x, w, tn, res=None, out_dtype=F32, wl=0, n_cols=None):
    m, k = x.shape
    n = w.shape[2] if n_cols is None else n_cols
    in_specs = [pl.BlockSpec((TM, k), lambda j, i: (i, 0)),
                pl.BlockSpec((None, k, tn), lambda j, i: (wl, 0, j))]
    args = [x, w]
    if res is not None:
        in_specs.append(pl.BlockSpec((TM, tn), lambda j, i: (i, j)))
        args.append(res)
    return pl.pallas_call(
        functools.partial(_mm_body, has_res=res is not None),
        out_shape=jax.ShapeDtypeStruct((m, n), out_dtype),
        grid=(n // tn, m // TM),
        in_specs=in_specs,
        out_specs=pl.BlockSpec((TM, tn), lambda j, i: (i, j)),
        scratch_shapes=[pltpu.VMEM((k, tn), BF16)],
        compiler_params=_cparams("parallel", "arbitrary"),
        name="matmul",
    )(*args)


def _mm_t_body(x_ref, w_ref, o_ref, wb_ref, *, wl):
    @pl.when(pl.program_id(1) == 0)
    def _():
        wb_ref[...] = w_ref[:, wl, :].astype(BF16)

    o_ref[...] = _nt_dot(x_ref[...], wb_ref[...])


def _matmul_t(x, w_t, tn, wl, n_cols=None):
    m, k = x.shape
    n, depth, _ = w_t.shape
    n = n if n_cols is None else n_cols
    return pl.pallas_call(
        functools.partial(_mm_t_body, wl=wl),
        out_shape=jax.ShapeDtypeStruct((m, n), F32),
        grid=(n // tn, m // TM),
        in_specs=[pl.BlockSpec((TM, k), lambda j, i: (i, 0)),
                  pl.BlockSpec((tn, depth, k), lambda j, i: (j, 0, 0))],
        out_specs=pl.BlockSpec((TM, tn), lambda j, i: (i, j)),
        scratch_shapes=[pltpu.VMEM((tn, k), BF16)],
        compiler_params=_cparams("parallel", "arbitrary"),
        name="matmul_t",
    )(x, w_t)


def _swiglu_in_body(x_ref, wg_ref, wu_ref, o_ref, wgb_ref, wub_ref):
    _cast_when(pl.program_id(1) == 0, [(wg_ref, wgb_ref), (wu_ref, wub_ref)])
    x = x_ref[...]
    g = jnp.dot(x, wgb_ref[...], preferred_element_type=F32)
    u = jnp.dot(x, wub_ref[...], preferred_element_type=F32)
    o_ref[...] = (g * jax.nn.sigmoid(g) * u).astype(o_ref.dtype)


def _swiglu_in(x, w_in, tn, wl=0):
    m, k = x.shape
    f = w_in.shape[2] // 2
    nj = f // tn
    return pl.pallas_call(
        _swiglu_in_body,
        out_shape=jax.ShapeDtypeStruct((m, f), BF16),
        grid=(nj, m // TM),
        in_specs=[pl.BlockSpec((TM, k), lambda j, i: (i, 0)),
                  pl.BlockSpec((None, k, tn), lambda j, i: (wl, 0, j)),
                  pl.BlockSpec((None, k, tn), lambda j, i: (wl, 0, nj + j))],
        out_specs=pl.BlockSpec((TM, tn), lambda j, i: (i, j)),
        scratch_shapes=[pltpu.VMEM((k, tn), BF16), pltpu.VMEM((k, tn), BF16)],
        compiler_params=_cparams("parallel", "arbitrary"),
        name="swiglu_in",
    )(x, w_in, w_in)


def _merge_body(*refs):
    h_ref = refs[0]
    br_refs = refs[1:1 + N_BRANCH]
    wg_refs = refs[1 + N_BRANCH:1 + 2 * N_BRANCH]
    wb_refs = refs[1 + 2 * N_BRANCH:1 + 3 * N_BRANCH]
    bg_refs = refs[1 + 3 * N_BRANCH:1 + 4 * N_BRANCH]
    o_ref = refs[1 + 4 * N_BRANCH]
    wgb_refs = refs[2 + 4 * N_BRANCH:2 + 5 * N_BRANCH]
    wbb_refs = refs[2 + 5 * N_BRANCH:2 + 6 * N_BRANCH]
    _cast_when(pl.program_id(1) == 0, list(zip(wg_refs + wb_refs, wgb_refs + wbb_refs)))
    h = h_ref[...]
    acc = None
    for n in range(N_BRANCH):
        gate = jax.nn.sigmoid(jnp.dot(h, wgb_refs[n][...], preferred_element_type=F32) + bg_refs[n][...])
        o = jnp.dot(br_refs[n][...], wbb_refs[n][...], preferred_element_type=F32)
        acc = gate * o if acc is None else acc + gate * o
    o_ref[...] = acc.astype(o_ref.dtype)


def _merge(h, branches, w_gate, w_branch, b_gate, tn, wl):
    m, d = h.shape
    nj = d // tn
    tok = lambda j, i: (i, 0)
    in_specs = [pl.BlockSpec((TM, d), tok)]
    in_specs += [pl.BlockSpec((TM, W_MIX), tok) for _ in range(N_BRANCH)]
    in_specs += [pl.BlockSpec((None, d, tn), functools.partial(lambda n, j, i: (wl, 0, n * nj + j), n))
                 for n in range(N_BRANCH)]
    in_specs += [pl.BlockSpec((None, None, W_MIX, tn), functools.partial(lambda n, j, i: (wl, n, 0, j), n))
                 for n in range(N_BRANCH)]
    in_specs += [pl.BlockSpec((1, tn), functools.partial(lambda n, j, i: (0, n * nj + j), n))
                 for n in range(N_BRANCH)]
    return pl.pallas_call(
        _merge_body,
        out_shape=jax.ShapeDtypeStruct((m, d), BF16),
        grid=(nj, m // TM),
        in_specs=in_specs,
        out_specs=pl.BlockSpec((TM, tn), lambda j, i: (i, j)),
        scratch_shapes=([pltpu.VMEM((d, tn), BF16)] * N_BRANCH + [pltpu.VMEM((W_MIX, tn), BF16)] * N_BRANCH),
        compiler_params=_cparams("parallel", "arbitrary"),
        name="merge",
    )(h, *branches, *([w_gate] * N_BRANCH), *([w_branch] * N_BRANCH), *([b_gate] * N_BRANCH))


def _conv_prompt_body(ax_ref, ab_ref, ac_ref, w_ref, o_ref, st_ref, ext_ref, *, tt):
    i = pl.program_id(1)

    @pl.when(i == 0)
    def _():
        ext_ref[0:SUBLANES, :] = jnp.zeros((SUBLANES, W_MIX), F32)

    u = ac_ref[...] * ax_ref[...]
    ext_ref[SUBLANES:, :] = u
    w = w_ref[...]
    conv = (w[0:1] * ext_ref[SUBLANES - 2:SUBLANES - 2 + tt, :]
            + w[1:2] * ext_ref[SUBLANES - 1:SUBLANES - 1 + tt, :] + w[2:3] * u)
    o_ref[...] = (ab_ref[...] * conv).astype(o_ref.dtype)
    ext_ref[0:SUBLANES, :] = ext_ref[tt:tt + SUBLANES, :]

    @pl.when(i == pl.num_programs(1) - 1)
    def _():
        st_ref[0] = ext_ref[tt + SUBLANES - (CONV_W - 1):tt + SUBLANES, :]


def _conv_prompt(proj, conv_w, nb, t):
    tt = T_SEQ
    nt = t // tt
    cb = lambda c: (lambda b, i: (b * nt + i, c // W_MIX))
    return pl.pallas_call(
        functools.partial(_conv_prompt_body, tt=tt),
        out_shape=(jax.ShapeDtypeStruct((proj.shape[0], W_MIX), BF16),
                   jax.ShapeDtypeStruct((nb, CONV_W - 1, W_MIX), F32)),
        grid=(nb, nt),
        in_specs=[pl.BlockSpec((tt, W_MIX), cb(C_AX)),
                  pl.BlockSpec((tt, W_MIX), cb(C_AB)),
                  pl.BlockSpec((tt, W_MIX), cb(C_AC)),
                  pl.BlockSpec((CONV_W, W_MIX), lambda b, i: (0, 0))],
        out_specs=(pl.BlockSpec((tt, W_MIX), lambda b, i: (b * nt + i, 0)),
                   pl.BlockSpec((1, CONV_W - 1, W_MIX), lambda b, i: (b, 0, 0))),
        scratch_shapes=[pltpu.VMEM((tt + SUBLANES, W_MIX), F32)],
        compiler_params=_cparams("parallel", "arbitrary"),
        name="conv_prompt",
    )(proj, proj, proj, conv_w)


def _conv_sample_body(ax_ref, ab_ref, ac_ref, st_ref, w_ref, o_ref, nst_ref, *, t):
    w = w_ref[...]
    u = [ac_ref[j] * ax_ref[j] for j in range(t)]
    ext = [st_ref[j] for j in range(CONV_W - 1)] + u
    for j in range(t):
        conv = w[0:1] * ext[j]
        for c in range(1, CONV_W):
            conv = conv + w[c:c + 1] * ext[j + c]
        o_ref[j] = ab_ref[j] * conv
    for j in range(CONV_W - 1):
        nst_ref[j] = ext[t + j]


def _conv_sample(ax, ab, ac, state, conv_w):
    t, nb, _ = ax.shape
    return pl.pallas_call(
        functools.partial(_conv_sample_body, t=t),
        out_shape=(jax.ShapeDtypeStruct((t, nb, W_MIX), F32),
                   jax.ShapeDtypeStruct((CONV_W - 1, nb, W_MIX), F32)),
        name="conv_sample",
    )(ax, ab, ac, state, conv_w)


def _pool_prompt_body(c_ref, pw_ref, ps_ref, o_ref, st_ref, ext_ref, *, tt):
    i = pl.program_id(1)
    hist = 2 * SUBLANES

    @pl.when(i == 0)
    def _():
        ext_ref[0:hist, :] = jnp.zeros((hist, W_MIX), F32)

    c = c_ref[...]
    ext_ref[hist:, :] = c
    pos = i * tt + lax.broadcasted_iota(jnp.int32, (tt, 1), 0)
    for g, win in enumerate(POOL_WINDOWS):
        cols = slice(g * GC, (g + 1) * GC)
        cg = c[:, cols]
        s = cg
        for j in range(1, win):
            s = s + ext_ref[hist - j:hist - j + tt, cols]
        cnt = jnp.minimum(win, pos + 1).astype(F32)
        pooled = s / cnt - cg
        og = jnp.dot(pooled.astype(BF16), pw_ref[g], preferred_element_type=F32)
        o_ref[:, cols] = (og * ps_ref[:, cols]).astype(o_ref.dtype)
    ext_ref[0:hist, :] = ext_ref[tt:tt + hist, :]

    @pl.when(i == pl.num_programs(1) - 1)
    def _():
        st_ref[0] = ext_ref[tt + hist - POOL_BUF:tt + hist, :]


def _pool_prompt(proj, pool_w, pool_scale, nb, t):
    tt = T_SEQ
    nt = t // tt
    return pl.pallas_call(
        functools.partial(_pool_prompt_body, tt=tt),
        out_shape=(jax.ShapeDtypeStruct((proj.shape[0], W_MIX), BF16),
                   jax.ShapeDtypeStruct((nb, POOL_BUF, W_MIX), F32)),
        grid=(nb, nt),
        in_specs=[pl.BlockSpec((tt, W_MIX), lambda b, i: (b * nt + i, C_CX // W_MIX)),
                  pl.BlockSpec((len(POOL_WINDOWS), GC, GC), lambda b, i: (0, 0, 0)),
                  pl.BlockSpec((1, W_MIX), lambda b, i: (0, 0))],
        out_specs=(pl.BlockSpec((tt, W_MIX), lambda b, i: (b * nt + i, 0)),
                   pl.BlockSpec((1, POOL_BUF, W_MIX), lambda b, i: (b, 0, 0))),
        scratch_shapes=[pltpu.VMEM((tt + 2 * SUBLANES, W_MIX), F32)],
        compiler_params=_cparams("parallel", "arbitrary"),
        name="pool_prompt",
    )(proj, pool_w, pool_scale)


def _pool_sample_body(c_ref, st_ref, pw_ref, ps_ref, o_ref, nst_ref, *, t, past_len):
    ext = [st_ref[j] for j in range(POOL_BUF)] + [c_ref[j] for j in range(t)]
    for g, win in enumerate(POOL_WINDOWS):
        cols = slice(g * GC, (g + 1) * GC)
        rows = []
        for j in range(t):
            s = ext[POOL_BUF + j][:, cols]
            for k in range(1, win):
                s = s + ext[POOL_BUF + j - k][:, cols]
            cnt = float(min(win, past_len + j + 1))
            rows.append(s / cnt - ext[POOL_BUF + j][:, cols])
        pooled = jnp.concatenate(rows, axis=0)
        og = jnp.dot(pooled.astype(BF16), pw_ref[g], preferred_element_type=F32) * ps_ref[:, cols]
        nb = og.shape[0] // t
        for j in range(t):
            o_ref[j, :, cols] = og[j * nb:(j + 1) * nb]
    for j in range(POOL_BUF):
        nst_ref[j] = ext[t + j]


def _pool_sample(c, state, pool_w, pool_scale, past_len):
    t, nb, _ = c.shape
    return pl.pallas_call(
        functools.partial(_pool_sample_body, t=t, past_len=past_len),
        out_shape=(jax.ShapeDtypeStruct((t, nb, W_MIX), F32),
                   jax.ShapeDtypeStruct((POOL_BUF, nb, W_MIX), F32)),
        name="pool_sample",
    )(c, state, pool_w, pool_scale)


def _upper_ones(n):
    r = lax.broadcasted_iota(jnp.int32, (n, n), 0)
    c = lax.broadcasted_iota(jnp.int32, (n, n), 1)
    return (r <= c).astype(F32)


def _cumsum_body(lf_ref, o_ref, carry_ref, *, tc):
    @pl.when(pl.program_id(1) == 0)
    def _():
        carry_ref[...] = jnp.zeros_like(carry_ref)

    f = jnp.dot(lf_ref[...], _upper_ones(tc), preferred_element_type=F32,
                precision=lax.Precision.HIGHEST) + carry_ref[...]
    o_ref[...] = f * LOG2E
    carry_ref[...] = f[:, tc - 1:tc]


def _cumsum_prompt(lf_t, nb, t):
    tc = T_SEQ
    nt = t // tc
    return pl.pallas_call(
        functools.partial(_cumsum_body, tc=tc),
        out_shape=jax.ShapeDtypeStruct((SUBLANES, nb * t), F32),
        grid=(nb, nt),
        in_specs=[pl.BlockSpec((SUBLANES, tc), lambda b, i: (0, b * nt + i))],
        out_specs=pl.BlockSpec((SUBLANES, tc), lambda b, i: (0, b * nt + i)),
        scratch_shapes=[pltpu.VMEM((SUBLANES, 1), F32)],
        compiler_params=_cparams("parallel", "arbitrary"),
        name="cumsum_prompt",
    )(lf_t)


def _diff_lambda(dl_ref, lam_init):
    dl = dl_ref[...]
    a = jnp.sum(dl[0:1] * dl[1:2], axis=-1, keepdims=True)
    b = jnp.sum(dl[2:3] * dl[3:4], axis=-1, keepdims=True)
    return jnp.exp(a) - jnp.exp(b) + lam_init


def _half_masks(shape):
    lane = lax.broadcasted_iota(jnp.int32, shape, len(shape) - 1)
    return lane < DQK_D, lane >= DQK_D


def _softmax_step(s, v, m_ref, l_ref, acc_ref):
    m_prev = m_ref[...]
    m_new = jnp.maximum(m_prev, jnp.max(s, axis=-1, keepdims=True))
    alpha = jnp.exp2(m_prev - m_new)
    p = jnp.exp2(s - m_new)
    l_ref[...] = alpha * l_ref[...] + jnp.sum(p, axis=-1, keepdims=True)
    acc_ref[...] = alpha * acc_ref[...] + jnp.dot(p.astype(BF16), v, preferred_element_type=F32)
    m_ref[...] = m_new


def _attn_prompt_body(*refs, diff, lam_init, tq):
    n_soft = 2 if diff else 1
    n_state = n_soft * H_ATT
    n_in = 5 if diff else 4
    if diff:
        q_ref, k_ref, v_ref, dl_ref, sub_ref, o_ref = refs[:n_in + 1]
    else:
        q_ref, k_ref, v_ref, f_ref, o_ref = refs[:n_in + 1]
    m_refs = refs[n_in + 1:n_in + 1 + n_state]
    l_refs = refs[n_in + 1 + n_state:n_in + 1 + 2 * n_state]
    acc_refs = refs[n_in + 1 + 2 * n_state:]
    qi = pl.program_id(1)
    ki = pl.program_id(2)
    scale = (DIFF_SCALE if diff else FOX_SCALE) * LOG2E

    @pl.when(ki == 0)
    def _():
        for m_ref, l_ref, acc_ref in zip(m_refs, l_refs, acc_refs):
            m_ref[...] = jnp.full(m_ref.shape, NEG_INF, F32)
            l_ref[...] = jnp.zeros(l_ref.shape, F32)
            acc_ref[...] = jnp.zeros(acc_ref.shape, F32)

    def block(masked):
        if masked:
            causal = (lax.broadcasted_iota(jnp.int32, (tq, tq), 1)
                      <= lax.broadcasted_iota(jnp.int32, (tq, tq), 0))
        def scores(h, n):
            cs = slice(h * DH, (h + 1) * DH)
            q = q_ref[:, cs] * scale
            if diff:
                q = jnp.where(_half_masks(q.shape)[n], q, 0.0)
            s = _nt_dot(q.astype(BF16), k_ref[:, cs].astype(BF16))
            if not diff:
                s = s - f_ref[h:h + 1, :]
            if masked:
                s = jnp.where(causal, s, NEG_INF)
            return s

        items = [(h, n) for h in range(H_ATT) for n in range(n_soft)]
        s_next = scores(*items[0])
        for pos, (h, n) in enumerate(items):
            s = s_next
            if pos + 1 < len(items):
                s_next = scores(*items[pos + 1])
            v = v_ref[:, h * DH:(h + 1) * DH].astype(BF16)
            idx = n * H_ATT + h
            _softmax_step(s, v, m_refs[idx], l_refs[idx], acc_refs[idx])

    @pl.when(ki < qi)
    def _():
        block(False)

    @pl.when(ki == qi)
    def _():
        block(True)
        if diff:
            lam = _diff_lambda(dl_ref, lam_init)
        for h in range(H_ATT):
            o = acc_refs[h][...] / l_refs[h][...]
            if diff:
                o = o - lam * (acc_refs[H_ATT + h][...] / l_refs[H_ATT + h][...])
                o = _rms(o, sub_ref[...]) * (1.0 - lam_init)
            o_ref[:, h * DH:(h + 1) * DH] = o.astype(o_ref.dtype)


def _attn_prompt(proj, nb, t, c_q, c_k, c_v, *, f_cum=None, diff_lambda=None, diff_subln=None,
                 lam_init=0.0):
    diff = f_cum is None
    tq = TQ
    nt = t // tq
    wide = H_ATT * DH
    n_state = (2 if diff else 1) * H_ATT
    in_specs = [pl.BlockSpec((tq, wide), lambda b, qi, ki: (b * nt + qi, c_q // wide)),
                pl.BlockSpec((tq, wide), lambda b, qi, ki: (b * nt + jnp.minimum(ki, qi), c_k // wide)),
                pl.BlockSpec((tq, wide), lambda b, qi, ki: (b * nt + jnp.minimum(ki, qi), c_v // wide))]
    args = [proj, proj, proj]
    if diff:
        in_specs += [pl.BlockSpec((4, DQK_D), lambda b, qi, ki: (0, 0)),
                     pl.BlockSpec((1, DH), lambda b, qi, ki: (0, 0))]
        args += [diff_lambda, diff_subln.reshape(1, DH)]
    else:
        in_specs += [pl.BlockSpec((SUBLANES, tq), lambda b, qi, ki: (0, b * nt + jnp.minimum(ki, qi)))]
        args += [f_cum]
    return pl.pallas_call(
        functools.partial(_attn_prompt_body, diff=diff, lam_init=lam_init, tq=tq),
        out_shape=jax.ShapeDtypeStruct((proj.shape[0], wide), BF16),
        grid=(nb, nt, nt),
        in_specs=in_specs,
        out_specs=pl.BlockSpec((tq, wide), lambda b, qi, ki: (b * nt + qi, 0)),
        scratch_shapes=([pltpu.VMEM((tq, 1), F32)] * (2 * n_state) + [pltpu.VMEM((tq, DH), F32)] * n_state),
        compiler_params=_cparams("parallel", "parallel", "arbitrary"),
        name="attn_prompt_diff" if diff else "attn_prompt_fox",
    )(*args)


def _attn_sample_body(*refs, diff, lam_init, t):
    np_ = PAGES_PER_STEP
    pt_ref, q_ref, kn_ref, vn_ref = refs[:4]
    if diff:
        dl_ref, sub_ref = refs[4:6]
        pages = refs[6:6 + np_]
        o_ref, m_ref, l_ref, acc_ref = refs[6 + np_:]
    else:
        lfn_ref = refs[4]
        pages = refs[5:5 + np_]
        lf_pages = refs[5 + np_:5 + 2 * np_]
        o_ref, m_ref, l_ref, acc_ref, carry_ref = refs[5 + 2 * np_:]
    del pt_ref
    step = pl.program_id(1)
    last = pl.num_programs(1) - 1
    r = 2 * t if diff else t
    scale = (DIFF_SCALE if diff else FOX_SCALE) * LOG2E
    row_stride = 2 * H_ATT

    @pl.when(step == 0)
    def _():
        m_ref[...] = jnp.full(m_ref.shape, NEG_INF, F32)
        l_ref[...] = jnp.zeros(l_ref.shape, F32)
        acc_ref[...] = jnp.zeros(acc_ref.shape, F32)
        if not diff:
            carry_ref[...] = jnp.zeros(carry_ref.shape, F32)

    q_all = q_ref[0] * scale

    def head_q(h):
        qh = q_all[:, h * DH:(h + 1) * DH]
        if diff:
            lo, hi = _half_masks(qh.shape)
            qh = jnp.concatenate([jnp.where(lo, qh, 0.0), jnp.where(hi, qh, 0.0)], axis=0)
        return qh

    def per_head(x, first, stride, rows):
        return jnp.concatenate(
            [jnp.broadcast_to(x[first + h * stride:first + h * stride + 1], (rows, x.shape[1]))
             for h in range(H_ATT)], axis=0)

    def cache_rows(first):
        return jnp.concatenate([pg[pl.ds(first, PAGE_SIZE, stride=row_stride), :] for pg in pages],
                               axis=0).astype(BF16)

    if not diff:
        lf = jnp.concatenate([lp[pl.ds(h, 1), :] for h in range(H_ATT) for lp in lf_pages], axis=0)
        n_lf = H_ATT * np_
        ra = lax.broadcasted_iota(jnp.int32, (n_lf, n_lf), 0)
        rb = lax.broadcasted_iota(jnp.int32, (n_lf, n_lf), 1)
        earlier = ((ra // np_ == rb // np_) & (rb < ra)).astype(F32)
        prev = jnp.dot(earlier, lf, preferred_element_type=F32, precision=lax.Precision.HIGHEST)
        rr = lax.broadcasted_iota(jnp.int32, (2 * PAGE_SIZE, PAGE_SIZE), 0)
        cc = lax.broadcasted_iota(jnp.int32, (2 * PAGE_SIZE, PAGE_SIZE), 1)
        tri_ones = ((rr <= cc) | (rr >= PAGE_SIZE)).astype(F32)
        f = jnp.dot(jnp.concatenate([lf, prev], axis=1), tri_ones, preferred_element_type=F32,
                    precision=lax.Precision.HIGHEST) + carry_ref[...]
        carry_ref[...] = per_head(f[:, PAGE_SIZE - 1:PAGE_SIZE], np_ - 1, np_, np_)
        f = f * LOG2E
        bias = jnp.concatenate(
            [jnp.broadcast_to(jnp.concatenate([f[h * np_ + i:h * np_ + i + 1] for i in range(np_)], axis=1),
                              (r, np_ * PAGE_SIZE)) for h in range(H_ATT)], axis=0)
    s = jnp.concatenate([_nt_dot(head_q(h).astype(BF16), cache_rows(h)) for h in range(H_ATT)],
                        axis=0)
    if not diff:
        s = s - bias
    m_prev = m_ref[...]
    m_new = jnp.maximum(m_prev, jnp.max(s, axis=-1, keepdims=True))
    alpha = jnp.exp2(m_prev - m_new)
    p = jnp.exp2(s - m_new)
    l_ref[...] = alpha * l_ref[...] + jnp.sum(p, axis=-1, keepdims=True)
    pv = jnp.concatenate([jnp.dot(p[h * r:(h + 1) * r].astype(BF16), cache_rows(H_ATT + h),
                                  preferred_element_type=F32) for h in range(H_ATT)], axis=0)
    acc_ref[...] = alpha * acc_ref[...] + pv
    m_ref[...] = m_new

    @pl.when(step == last)
    def _():
        qrow = lax.broadcasted_iota(jnp.int32, (H_ATT * r, 1), 0) % t
        qh = [head_q(h) for h in range(H_ATT)]
        kn = kn_ref[0]
        vn = vn_ref[0]
        m = m_ref[...]
        l = l_ref[...]
        acc = acc_ref[...]
        if not diff:
            off = per_head(carry_ref[...], 0, np_, r)
            lfn = lfn_ref[0]
        for j in range(t):
            sj = jnp.concatenate(
                [jnp.sum(qh[h] * kn[j:j + 1, h * DH:(h + 1) * DH], axis=-1, keepdims=True)
                 for h in range(H_ATT)], axis=0)
            if not diff:
                off = off + jnp.concatenate(
                    [jnp.broadcast_to(lfn[h:h + 1, j:j + 1], (r, 1)) for h in range(H_ATT)], axis=0)
                sj = sj - off * LOG2E
            sj = jnp.where(qrow >= j, sj, NEG_INF)
            m_new = jnp.maximum(m, sj)
            alpha = jnp.exp2(m - m_new)
            pj = jnp.exp2(sj - m_new)
            l = alpha * l + pj
            vj = jnp.concatenate([jnp.broadcast_to(vn[j:j + 1, h * DH:(h + 1) * DH], (r, DH))
                                  for h in range(H_ATT)], axis=0)
            acc = alpha * acc + pj * vj
            m = m_new
        o_all = acc / l
        if diff:
            lam = _diff_lambda(dl_ref, lam_init)
        for h in range(H_ATT):
            o = o_all[h * r:(h + 1) * r]
            if diff:
                o = o[0:t] - lam * o[t:2 * t]
                o = _rms(o, sub_ref[...]) * (1.0 - lam_init)
            o_ref[0, :, h * DH:(h + 1) * DH] = o


def _attn_sample(proj_s, cache, page_table, layer, c_q, c_k, c_v, *, lf_new=None, cache_lf=None,
                 diff_lambda=None, diff_subln=None, lam_init=0.0):
    diff = cache_lf is None
    nb, t, _ = proj_s.shape
    n_pages = page_table.shape[1]
    np_ = PAGES_PER_STEP
    n_steps = n_pages // np_
    wide = H_ATT * DH
    r = 2 * t if diff else t

    def page_map(i):
        return lambda b, s, pt: (layer, pt[b * n_pages + s * np_ + i], 0, 0)

    in_specs = [pl.BlockSpec((1, t, wide), lambda b, s, pt: (b, 0, c_q // wide)),
                pl.BlockSpec((1, t, wide), lambda b, s, pt: (b, 0, c_k // wide)),
                pl.BlockSpec((1, t, wide), lambda b, s, pt: (b, 0, c_v // wide))]
    args = [proj_s, proj_s, proj_s]
    if diff:
        in_specs += [pl.BlockSpec((4, DQK_D), lambda b, s, pt: (0, 0)),
                     pl.BlockSpec((1, DH), lambda b, s, pt: (0, 0))]
        args += [diff_lambda, diff_subln.reshape(1, DH)]
    else:
        in_specs += [pl.BlockSpec((1, SUBLANES, t), lambda b, s, pt: (b, 0, 0))]
        args += [lf_new]
    in_specs += [pl.BlockSpec((None, None, PAGE_SIZE * 2 * H_ATT, DH), page_map(i)) for i in range(np_)]
    args += [cache] * np_
    scratch = [pltpu.VMEM((H_ATT * r, 1), F32), pltpu.VMEM((H_ATT * r, 1), F32),
               pltpu.VMEM((H_ATT * r, DH), F32)]
    if not diff:
        in_specs += [pl.BlockSpec((None, None, H_ATT, PAGE_SIZE), page_map(i)) for i in range(np_)]
        args += [cache_lf] * np_
        scratch += [pltpu.VMEM((H_ATT * np_, 1), F32)]
    return pl.pallas_call(
        functools.partial(_attn_sample_body, diff=diff, lam_init=lam_init, t=t),
        out_shape=jax.ShapeDtypeStruct((nb, t, wide), F32),
        grid_spec=pltpu.PrefetchScalarGridSpec(
            num_scalar_prefetch=1,
            grid=(nb, n_steps),
            in_specs=in_specs,
            out_specs=pl.BlockSpec((1, t, wide), lambda b, s, pt: (b, 0, 0)),
            scratch_shapes=scratch),
        compiler_params=_cparams("parallel", "arbitrary"),
        name="attn_sample_diff" if diff else "attn_sample_fox",
    )(page_table.reshape(-1), *args)


def _new_expert(te_ref):
    i = pl.program_id(1)
    return (i == 0) | (te_ref[i] != te_ref[jnp.maximum(i - 1, 0)])


def _moe_in_body(te_ref, nt_ref, x_ref, wg_ref, wu_ref, o_ref, wgb_ref, wub_ref):
    _cast_when(_new_expert(te_ref), [(wg_ref, wgb_ref), (wu_ref, wub_ref)])

    @pl.when(pl.program_id(1) < nt_ref[0])
    def _():
        x = x_ref[...]
        g = jnp.dot(x, wgb_ref[...], preferred_element_type=F32)
        u = jnp.dot(x, wub_ref[...], preferred_element_type=F32)
        o_ref[...] = (g * jax.nn.sigmoid(g) * u).astype(o_ref.dtype)

    @pl.when(pl.program_id(1) >= nt_ref[0])
    def _():
        o_ref[...] = jnp.zeros(o_ref.shape, o_ref.dtype)


def _moe_in(xs, w_in, tile_expert, n_tiles, wl):
    p, d = xs.shape
    f = w_in.shape[3] // 2
    nj = f // TN_E
    return pl.pallas_call(
        _moe_in_body,
        out_shape=jax.ShapeDtypeStruct((p, f), BF16),
        grid_spec=pltpu.PrefetchScalarGridSpec(
            num_scalar_prefetch=2,
            grid=(nj, p // TM_E),
            in_specs=[pl.BlockSpec((TM_E, d), lambda j, i, te, nt: (i, 0)),
                      pl.BlockSpec((None, None, d, TN_E), lambda j, i, te, nt: (wl, te[i], 0, j)),
                      pl.BlockSpec((None, None, d, TN_E), lambda j, i, te, nt: (wl, te[i], 0, nj + j))],
            out_specs=pl.BlockSpec((TM_E, TN_E), lambda j, i, te, nt: (i, j)),
            scratch_shapes=[pltpu.VMEM((d, TN_E), BF16), pltpu.VMEM((d, TN_E), BF16)]),
        compiler_params=_cparams("parallel", "arbitrary"),
        name="moe_in",
    )(tile_expert, n_tiles, xs, w_in, w_in)


def _moe_out_body(te_ref, nt_ref, a_ref, w_ref, o_ref, wb_ref):
    _cast_when(_new_expert(te_ref), [(w_ref, wb_ref)])

    @pl.when(pl.program_id(1) < nt_ref[0])
    def _():
        o_ref[...] = jnp.dot(a_ref[...], wb_ref[...], preferred_element_type=F32)

    @pl.when(pl.program_id(1) >= nt_ref[0])
    def _():
        o_ref[...] = jnp.zeros(o_ref.shape, o_ref.dtype)


def _moe_out(act, w_out, tile_expert, n_tiles, tn, wl):
    p, f = act.shape
    d = w_out.shape[3]
    return pl.pallas_call(
        _moe_out_body,
        out_shape=jax.ShapeDtypeStruct((p, d), F32),
        grid_spec=pltpu.PrefetchScalarGridSpec(
            num_scalar_prefetch=2,
            grid=(d // tn, p // TM_E),
            in_specs=[pl.BlockSpec((TM_E, f), lambda j, i, te, nt: (i, 0)),
                      pl.BlockSpec((None, None, f, tn), lambda j, i, te, nt: (wl, te[i], 0, j))],
            out_specs=pl.BlockSpec((TM_E, tn), lambda j, i, te, nt: (i, j)),
            scratch_shapes=[pltpu.VMEM((f, tn), BF16)]),
        compiler_params=_cparams("parallel", "arbitrary"),
        name="moe_out",
    )(tile_expert, n_tiles, act, w_out)


def _combine_body(x_ref, g0_ref, g1_ref, w_ref, gn_ref, o_ref, *, final_norm):
    w = w_ref[...]
    y = x_ref[...] + w[:, 0:1] * g0_ref[...] + w[:, 1:2] * g1_ref[...]
    if final_norm:
        y = _rms(y, gn_ref[...])
    o_ref[...] = y


def _combine(x, g, wts, g_final=None):
    n, d = x.shape
    tm = TM // 2
    final_norm = g_final is not None
    gn = (g_final if final_norm else jnp.ones((d,), F32)).reshape(1, d)
    return pl.pallas_call(
        functools.partial(_combine_body, final_norm=final_norm),
        out_shape=jax.ShapeDtypeStruct((n, d), F32),
        grid=(n // tm,),
        in_specs=[pl.BlockSpec((tm, d), lambda i: (i, 0)),
                  pl.BlockSpec((None, tm, d), lambda i: (0, i, 0)),
                  pl.BlockSpec((None, tm, d), lambda i: (1, i, 0)),
                  pl.BlockSpec((tm, LANES), lambda i: (i, 0)),
                  pl.BlockSpec((1, d), lambda i: (0, 0))],
        out_specs=pl.BlockSpec((tm, d), lambda i: (i, 0)),
        compiler_params=_cparams("parallel"),
        name="moe_combine",
    )(x, g, g, wts, gn)


def _moe(x, g_norm, router, w_in, w_out, wl, g_final=None):
    n, d = x.shape
    router_pad = jnp.pad(router, ((0, 0), (0, LANES - N_EXP)))
    h, ids, wts, rank, cnt = _norm_route(x, g_norm, router_pad)
    p_total = n * TOP_K + N_EXP * TM_E
    counts = cnt[0, :N_EXP].astype(jnp.int32)
    padded = (counts + TM_E - 1) // TM_E * TM_E
    ends = jnp.cumsum(padded)
    starts = ends - padded
    ids_k = ids[:, :TOP_K].T
    group_start = jnp.sum(jnp.where(ids_k[:, :, None] == jnp.arange(N_EXP, dtype=jnp.int32), starts, 0), axis=-1)
    dest = group_start + rank[:, :TOP_K].T
    tok = jnp.broadcast_to(jnp.arange(n, dtype=jnp.int32), (TOP_K, n))
    src_tok = jnp.zeros((p_total,), jnp.int32).at[dest.reshape(-1)].set(tok.reshape(-1))
    n_tiles = (ends[-1] // TM_E).astype(jnp.int32).reshape(1)
    tile_start = jnp.arange(p_total // TM_E, dtype=jnp.int32) * TM_E
    tile_expert = jnp.minimum(jnp.sum((tile_start[:, None] >= ends[None, :]).astype(jnp.int32), axis=1),
                              N_EXP - 1)
    last_valid = jnp.maximum(n_tiles[0] - 1, 0)
    tile_expert = jnp.where(tile_start // TM_E < n_tiles[0], tile_expert, tile_expert[last_valid])

    xs = jnp.take(h, src_tok, axis=0)
    act = _moe_in(xs, w_in, tile_expert, n_tiles, wl)
    ys = _moe_out(act, w_out, tile_expert, n_tiles, 512, wl)
    g = jnp.take(ys, dest, axis=0)
    return _combine(x, g, wts, g_final)


def _token_mixers(x, layer, lam_init, nb_p, t_p, nb_s, t_s, caches, states, page_table, lp):
    (g_mix, w_in_t, w_in_hi_t, b_forget, conv_w, pool_w, pool_scale, diff_lambda, diff_subln,
     w_branch, w_gate, b_gate, w_o) = lp
    cache_fox, cache_lf_t, cache_diff = caches
    state_conv, state_pool = states
    n_p = nb_p * t_p
    past_len = page_table.shape[1] * PAGE_SIZE

    wf_t = jnp.pad(w_in_t[N_LO:N_LO + H_ATT, layer, :], ((0, SUBLANES - H_ATT), (0, 0)))
    bf = jnp.pad(b_forget, (0, SUBLANES - H_ATT)).reshape(SUBLANES, 1)

    h, lf_t = _norm_forget(x, g_mix, wf_t, bf)
    lo = _matmul_t(h, w_in_t, 512, layer, n_cols=N_LO)
    hi = _matmul_t(h, w_in_hi_t, 512, layer)

    out_a, conv_p = _conv_prompt(lo, conv_w, nb_p, t_p)
    pool_w_b = pool_w.astype(BF16)
    pool_scale2 = pool_scale.reshape(1, W_MIX)
    out_c, pool_p = _pool_prompt(hi, pool_w_b, pool_scale2, nb_p, t_p)
    f_cum = _cumsum_prompt(lf_t, nb_p, t_p)
    out_b = _attn_prompt(lo, nb_p, t_p, C_FQ, C_FK, C_FV, f_cum=f_cum)
    out_d = _attn_prompt(hi, nb_p, t_p, C_DQ, C_DK, C_DV, diff_lambda=diff_lambda,
                         diff_subln=diff_subln, lam_init=lam_init)

    lo_s = lo[n_p:].reshape(nb_s, t_s, N_LO)
    hi_s = hi[n_p:].reshape(nb_s, t_s, N_HI)
    tmaj = lambda a, c: a[:, :, c:c + W_MIX].transpose(1, 0, 2)
    out_a_s, conv_s = _conv_sample(tmaj(lo_s, C_AX), tmaj(lo_s, C_AB), tmaj(lo_s, C_AC),
                                   state_conv.transpose(1, 0, 2), conv_w)
    out_c_s, pool_s = _pool_sample(tmaj(hi_s, C_CX), state_pool.transpose(1, 0, 2), pool_w_b, pool_scale2,
                                   past_len)
    lf_new = lf_t[:, n_p:].reshape(SUBLANES, nb_s, t_s).transpose(1, 0, 2)
    out_b_s = _attn_sample(lo_s, cache_fox, page_table, layer, C_FQ, C_FK, C_FV,
                           lf_new=lf_new, cache_lf=cache_lf_t)
    out_d_s = _attn_sample(hi_s, cache_diff, page_table, layer, C_DQ, C_DK, C_DV,
                           diff_lambda=diff_lambda, diff_subln=diff_subln, lam_init=lam_init)

    def with_sample(p, s):
        return lax.dynamic_update_slice(p, s.reshape(nb_s * t_s, W_MIX).astype(BF16), (n_p, 0))

    branches = [with_sample(out_a, out_a_s.transpose(1, 0, 2)), with_sample(out_b, out_b_s),
                with_sample(out_c, out_c_s.transpose(1, 0, 2)), with_sample(out_d, out_d_s)]
    merged = _merge(h, branches, w_gate, w_branch, b_gate.reshape(1, N_BRANCH * D_MODEL), 256, layer)
    x = _matmul(merged, w_o, 1024, res=x, wl=layer)

    lf_rows = lf_t[:H_ATT].T
    kv = lambda a, rows, c, nb, t: a[rows, c:c + 2 * W_MIX].reshape(nb, t, 2, H_ATT, DH)
    rp, rs = slice(0, n_p), slice(n_p, None)
    st_p = (kv(lo, rp, C_FK, nb_p, t_p), lf_rows[rp].reshape(nb_p, t_p, H_ATT), kv(hi, rp, C_DK, nb_p, t_p),
            conv_p, pool_p)
    st_s = (kv(lo, rs, C_FK, nb_s, t_s), lf_rows[rs].reshape(nb_s, t_s, H_ATT), kv(hi, rs, C_DK, nb_s, t_s),
            conv_s.transpose(1, 0, 2), pool_s.transpose(1, 0, 2))
    return x, st_p, st_s


def kernel(x_prompt, x_sample, cache_fox_kv, cache_fox_logf, cache_diff_kv, state_conv, state_pool,
           page_table, norm_mix, w_in, b_forget, conv_w, pool_w, pool_scale, diff_lambda, diff_subln,
           w_branch, w_gate, b_gate, w_o, norm_ffn, ffn_w_in, ffn_w_out, moe_router, moe_w_in,
           moe_w_out, norm_final):
    nb_p, t_p, d = x_prompt.shape
    nb_s, t_s, _ = x_sample.shape
    n_p = nb_p * t_p
    depth, n_phys = cache_fox_kv.shape[:2]
    assert d == D_MODEL and depth == DEPTH
    assert (n_p + nb_s * t_s) % TM == 0 and t_p % T_SEQ == 0 and t_p % TQ == 0
    assert page_table.shape[1] % PAGES_PER_STEP == 0

    x = jnp.concatenate([x_prompt.reshape(n_p, d), x_sample.reshape(nb_s * t_s, d)], axis=0)
    caches = (cache_fox_kv.reshape(depth, n_phys, PAGE_SIZE * 2 * H_ATT, DH),
              cache_fox_logf.transpose(0, 1, 3, 2),
              cache_diff_kv.reshape(depth, n_phys, PAGE_SIZE * 2 * H_ATT, DH))
    w_in_t = w_in.transpose(2, 0, 1)
    w_in_hi_t = w_in_t[N_LO + H_ATT:]
    st_p, st_s = [], []
    for l in range(depth):
        lam_init = 0.8 - 0.6 * math.exp(-0.3 * l)
        lp = (norm_mix[l], w_in_t, w_in_hi_t, b_forget[l], conv_w[l], pool_w[l], pool_scale[l], diff_lambda[l],
              diff_subln[l], w_branch, w_gate, b_gate[l], w_o)
        x, sp, ss = _token_mixers(x, l, lam_init, nb_p, t_p, nb_s, t_s, caches,
                                  (state_conv[l], state_pool[l]), page_table, lp)
        st_p.append(sp)
        st_s.append(ss)
        j = l // 2
        if l % 2 == 0:
            hf = _norm_plain(x, norm_ffn[l], BF16)
            act = _swiglu_in(hf, ffn_w_in, 512, wl=j)
            x = _matmul(act, ffn_w_out, 512, res=x, wl=j)
        else:
            x = _moe(x, norm_ffn[l], moe_router[j], moe_w_in, moe_w_out, j,
                     g_final=norm_final if l == depth - 1 else None)
    y = x if depth % 2 == 0 else _norm_plain(x, norm_final, F32)
    outs = [y[:n_p].reshape(nb_p, t_p, d), y[n_p:].reshape(nb_s, t_s, d)]
    for sts in (st_p, st_s):
        for k in range(5):
            outs.append(jnp.stack([s[k] for s in sts]))
    return tuple(outs)
```

```python
import functools
import math

import jax
import jax.numpy as jnp
from jax import lax
from jax.experimental import pallas as pl
from jax.experimental.pallas import tpu as pltpu

F32 = jnp.float32
BF16 = jnp.bfloat16

D_MODEL = 2048
DEPTH = 2
PAGE_SIZE = 128
N_BRANCH = 4
W_MIX = D_MODEL // 4
CONV_W = 3
H_ATT = 4
DH = W_MIX // H_ATT
DQK_D = DH // 2
FOX_SCALE = DH ** -0.5
DIFF_SCALE = DQK_D ** -0.5
POOL_WINDOWS = (2, 4, 8, 16)
GC = W_MIX // len(POOL_WINDOWS)
POOL_BUF = max(POOL_WINDOWS) - 1
D_FF = 11 * D_MODEL // 4
N_EXP = 8
TOP_K = 2
D_EXP = 7 * D_MODEL // 2
EPS = 1e-6
NEG_INF = -1e30
LOG2E = math.log2(math.e)
C_AX, C_AB, C_AC, C_FQ, C_FK, C_FV = (i * W_MIX for i in range(6))
C_CX, C_DQ, C_DK, C_DV = (i * W_MIX for i in range(4))
N_LO = 6 * W_MIX
N_HI = 4 * W_MIX

LANES = 128
SUBLANES = 8
VMEM_LIMIT = 56 * 1024 * 1024

TM = 640
T_SEQ = 512
TQ = 512
PAGES_PER_STEP = 16
TM_E = 256
TN_E = 1024


def _cparams(*sem):
    return pltpu.CompilerParams(dimension_semantics=sem, vmem_limit_bytes=VMEM_LIMIT)


def _nt_dot(a, b, **kw):
    return lax.dot_general(a, b, (((1,), (1,)), ((), ())), preferred_element_type=F32, **kw)


def _rms(x, g):
    return x * lax.rsqrt(jnp.mean(x * x, axis=-1, keepdims=True) + EPS) * g


def _log_sigmoid(z):
    return jnp.minimum(z, 0.0) - jnp.log1p(jnp.exp(-jnp.abs(z)))


def _norm_plain_body(x_ref, g_ref, h_ref):
    h_ref[...] = _rms(x_ref[...], g_ref[...]).astype(h_ref.dtype)


def _norm_plain(x, g, out_dtype):
    n, d = x.shape
    return pl.pallas_call(
        _norm_plain_body,
        out_shape=jax.ShapeDtypeStruct((n, d), out_dtype),
        grid=(n // TM,),
        in_specs=[pl.BlockSpec((TM, d), lambda i: (i, 0)),
                  pl.BlockSpec((1, d), lambda i: (0, 0))],
        out_specs=pl.BlockSpec((TM, d), lambda i: (i, 0)),
        compiler_params=_cparams("parallel"),
        name="norm_plain",
    )(x, g.reshape(1, d))


def _norm_forget_body(x_ref, g_ref, wf_ref, bf_ref, h_ref, lf_ref):
    hb = _rms(x_ref[...], g_ref[...]).astype(BF16)
    h_ref[...] = hb
    lf_ref[...] = _log_sigmoid(_nt_dot(wf_ref[...].astype(BF16), hb) + bf_ref[...])


def _norm_forget(x, g, wf_t, bf):
    n, d = x.shape
    return pl.pallas_call(
        _norm_forget_body,
        out_shape=(jax.ShapeDtypeStruct((n, d), BF16), jax.ShapeDtypeStruct((SUBLANES, n), F32)),
        grid=(n // TM,),
        in_specs=[pl.BlockSpec((TM, d), lambda i: (i, 0)),
                  pl.BlockSpec((1, d), lambda i: (0, 0)),
                  pl.BlockSpec((SUBLANES, d), lambda i: (0, 0)),
                  pl.BlockSpec((SUBLANES, 1), lambda i: (0, 0))],
        out_specs=(pl.BlockSpec((TM, d), lambda i: (i, 0)),
                   pl.BlockSpec((SUBLANES, TM), lambda i: (0, i))),
        compiler_params=_cparams("parallel"),
        name="norm_forget",
    )(x, g.reshape(1, d), wf_t, bf)


def _norm_route_body(x_ref, g_ref, wr_ref, h_ref, ids_ref, wts_ref, rank_ref, cnt_ref, carry_ref):
    @pl.when(pl.program_id(0) == 0)
    def _():
        carry_ref[...] = jnp.zeros(carry_ref.shape, F32)

    h = _rms(x_ref[...], g_ref[...])
    h_ref[...] = h.astype(BF16)
    logits = jnp.dot(h, wr_ref[...], preferred_element_type=F32, precision=lax.Precision.HIGHEST)
    lane = lax.broadcasted_iota(jnp.int32, logits.shape, 1)
    logits = jnp.where(lane < N_EXP, logits, -jnp.inf)
    m1 = jnp.max(logits, axis=-1, keepdims=True)
    i1 = jnp.min(jnp.where(logits == m1, lane, LANES), axis=-1, keepdims=True)
    rest = jnp.where(lane == i1, -jnp.inf, logits)
    m2 = jnp.max(rest, axis=-1, keepdims=True)
    i2 = jnp.min(jnp.where(rest == m2, lane, LANES), axis=-1, keepdims=True)
    e2 = jnp.exp(m2 - m1)
    w1 = 1.0 / (1.0 + e2)
    w2 = e2 / (1.0 + e2)
    ids_ref[...] = jnp.where(lane == 0, i1, i2)
    wts_ref[...] = jnp.where(lane == 0, w1, w2)
    chosen = (lane == i1) | (lane == i2)
    tm = chosen.shape[0]
    earlier = (lax.broadcasted_iota(jnp.int32, (tm, tm), 1)
               < lax.broadcasted_iota(jnp.int32, (tm, tm), 0))
    before = jnp.dot(jnp.where(earlier, 1.0, 0.0).astype(BF16), jnp.where(chosen, 1.0, 0.0).astype(BF16),
                     preferred_element_type=F32) + carry_ref[...]
    r1 = jnp.sum(jnp.where(lane == i1, before, 0.0), axis=-1, keepdims=True)
    r2 = jnp.sum(jnp.where(lane == i2, before, 0.0), axis=-1, keepdims=True)
    rank_ref[...] = jnp.where(lane == 0, r1, r2).astype(jnp.int32)
    total = carry_ref[...] + jnp.sum(jnp.where(chosen, 1.0, 0.0), axis=0, keepdims=True)
    carry_ref[...] = total
    cnt_ref[...] = jnp.broadcast_to(total, cnt_ref.shape)


def _norm_route(x, g, router_pad):
    n, d = x.shape
    return pl.pallas_call(
        _norm_route_body,
        out_shape=(jax.ShapeDtypeStruct((n, d), BF16),
                   jax.ShapeDtypeStruct((n, LANES), jnp.int32),
                   jax.ShapeDtypeStruct((n, LANES), F32),
                   jax.ShapeDtypeStruct((n, LANES), jnp.int32),
                   jax.ShapeDtypeStruct((SUBLANES, LANES), F32)),
        grid=(n // TM,),
        in_specs=[pl.BlockSpec((TM, d), lambda i: (i, 0)),
                  pl.BlockSpec((1, d), lambda i: (0, 0)),
                  pl.BlockSpec((d, LANES), lambda i: (0, 0))],
        out_specs=(pl.BlockSpec((TM, d), lambda i: (i, 0)),
                   pl.BlockSpec((TM, LANES), lambda i: (i, 0)),
                   pl.BlockSpec((TM, LANES), lambda i: (i, 0)),
                   pl.BlockSpec((TM, LANES), lambda i: (i, 0)),
                   pl.BlockSpec((SUBLANES, LANES), lambda i: (0, 0))),
        scratch_shapes=[pltpu.VMEM((1, LANES), F32)],
        compiler_params=_cparams("arbitrary"),
        name="norm_route",
    )(x, g.reshape(1, d), router_pad)


def _cast_when(first, pairs):
    @pl.when(first)
    def _():
        for src_ref, dst_ref in pairs:
            dst_ref[...] = src_ref[...].reshape(dst_ref.shape).astype(BF16)


def _mm_body(x_ref, w_ref, *rest, has_res):
    r_ref = rest[0] if has_res else None
    o_ref, wb_ref = rest[-2:]
    _cast_when(pl.program_id(1) == 0, [(w_ref, wb_ref)])
    acc = jnp.dot(x_ref[...], wb_ref[...], preferred_element_type=F32)
    if has_res:
        acc = r_ref[...] + acc
    o_ref[...] = acc.astype(o_ref.dtype)


def _matmul(x, w, tn, res=None, out_dtype=F32, wl=0, n_cols=None):
    m, k = x.shape
    n = w.shape[2] if n_cols is None else n_cols
    in_specs = [pl.BlockSpec((TM, k), lambda j, i: (i, 0)),
                pl.BlockSpec((None, k, tn), lambda j, i: (wl, 0, j))]
    args = [x, w]
    if res is not None:
        in_specs.append(pl.BlockSpec((TM, tn), lambda j, i: (i, j)))
        args.append(res)
    return pl.pallas_call(
        functools.partial(_mm_body, has_res=res is not None),
        out_shape=jax.ShapeDtypeStruct((m, n), out_dtype),
        grid=(n // tn, m // TM),
        in_specs=in_specs,
        out_specs=pl.BlockSpec((TM, tn), lambda j, i: (i, j)),
        scratch_shapes=[pltpu.VMEM((k, tn), BF16)],
        compiler_params=_cparams("parallel", "arbitrary"),
        name="matmul",
    )(*args)


def _mm_t_body(x_ref, w_ref, o_ref, wb_ref, *, wl):
    @pl.when(pl.program_id(1) == 0)
    def _():
        wb_ref[...] = w_ref[:, wl, :].astype(BF16)

    o_ref[...] = _nt_dot(x_ref[...], wb_ref[...])


def _matmul_t(x, w_t, tn, wl, n_cols=None):
    m, k = x.shape
    n, depth, _ = w_t.shape
    n = n if n_cols is None else n_cols
    return pl.pallas_call(
        functools.partial(_mm_t_body, wl=wl),
        out_shape=jax.ShapeDtypeStruct((m, n), F32),
        grid=(n // tn, m // TM),
        in_specs=[pl.BlockSpec((TM, k), lambda j, i: (i, 0)),
                  pl.BlockSpec((tn, depth, k), lambda j, i: (j, 0, 0))],
        out_specs=pl.BlockSpec((TM, tn), lambda j, i: (i, j)),
        scratch_shapes=[pltpu.VMEM((tn, k), BF16)],
        compiler_params=_cparams("parallel", "arbitrary"),
        name="matmul_t",
    )(x, w_t)


def _swiglu_in_body(x_ref, wg_ref, wu_ref, o_ref, wgb_ref, wub_ref):
    _cast_when(pl.program_id(1) == 0, [(wg_ref, wgb_ref), (wu_ref, wub_ref)])
    x = x_ref[...]
    g = jnp.dot(x, wgb_ref[...], preferred_element_type=F32)
    u = jnp.dot(x, wub_ref[...], preferred_element_type=F32)
    o_ref[...] = (g * jax.nn.sigmoid(g) * u).astype(o_ref.dtype)


def _swiglu_in(x, w_in, tn, wl=0):
    m, k = x.shape
    f = w_in.shape[2] // 2
    nj = f // tn
    return pl.pallas_call(
        _swiglu_in_body,
        out_shape=jax.ShapeDtypeStruct((m, f), BF16),
        grid=(nj, m // TM),
        in_specs=[pl.BlockSpec((TM, k), lambda j, i: (i, 0)),
                  pl.BlockSpec((None, k, tn), lambda j, i: (wl, 0, j)),
                  pl.BlockSpec((None, k, tn), lambda j, i: (wl, 0, nj + j))],
        out_specs=pl.BlockSpec((TM, tn), lambda j, i: (i, j)),
        scratch_shapes=[pltpu.VMEM((k, tn), BF16), pltpu.VMEM((k, tn), BF16)],
        compiler_params=_cparams("parallel", "arbitrary"),
        name="swiglu_in",
    )(x, w_in, w_in)


def _merge_body(*refs):
    h_ref = refs[0]
    br_refs = refs[1:1 + N_BRANCH]
    wg_refs = refs[1 + N_BRANCH:1 + 2 * N_BRANCH]
    wb_refs = refs[1 + 2 * N_BRANCH:1 + 3 * N_BRANCH]
    bg_refs = refs[1 + 3 * N_BRANCH:1 + 4 * N_BRANCH]
    o_ref = refs[1 + 4 * N_BRANCH]
    wgb_refs = refs[2 + 4 * N_BRANCH:2 + 5 * N_BRANCH]
    wbb_refs = refs[2 + 5 * N_BRANCH:2 + 6 * N_BRANCH]
    _cast_when(pl.program_id(1) == 0, list(zip(wg_refs + wb_refs, wgb_refs + wbb_refs)))
    h = h_ref[...]
    acc = None
    for n in range(N_BRANCH):
        gate = jax.nn.sigmoid(jnp.dot(h, wgb_refs[n][...], preferred_element_type=F32) + bg_refs[n][...])
        o = jnp.dot(br_refs[n][...], wbb_refs[n][...], preferred_element_type=F32)
        acc = gate * o if acc is None else acc + gate * o
    o_ref[...] = acc.astype(o_ref.dtype)


def _merge(h, branches, w_gate, w_branch, b_gate, tn, wl):
    m, d = h.shape
    nj = d // tn
    tok = lambda j, i: (i, 0)
    in_specs = [pl.BlockSpec((TM, d), tok)]
    in_specs += [pl.BlockSpec((TM, W_MIX), tok) for _ in range(N_BRANCH)]
    in_specs += [pl.BlockSpec((None, d, tn), functools.partial(lambda n, j, i: (wl, 0, n * nj + j), n))
                 for n in range(N_BRANCH)]
    in_specs += [pl.BlockSpec((None, None, W_MIX, tn), functools.partial(lambda n, j, i: (wl, n, 0, j), n))
                 for n in range(N_BRANCH)]
    in_specs += [pl.BlockSpec((1, tn), functools.partial(lambda n, j, i: (0, n * nj + j), n))
                 for n in range(N_BRANCH)]
    return pl.pallas_call(
        _merge_body,
        out_shape=jax.ShapeDtypeStruct((m, d), BF16),
        grid=(nj, m // TM),
        in_specs=in_specs,
        out_specs=pl.BlockSpec((TM, tn), lambda j, i: (i, j)),
        scratch_shapes=([pltpu.VMEM((d, tn), BF16)] * N_BRANCH + [pltpu.VMEM((W_MIX, tn), BF16)] * N_BRANCH),
        compiler_params=_cparams("parallel", "arbitrary"),
        name="merge",
    )(h, *branches, *([w_gate] * N_BRANCH), *([w_branch] * N_BRANCH), *([b_gate] * N_BRANCH))


def _conv_prompt_body(ax_ref, ab_ref, ac_ref, w_ref, o_ref, st_ref, ext_ref, *, tt):
    i = pl.program_id(1)

    @pl.when(i == 0)
    def _():
        ext_ref[0:SUBLANES, :] = jnp.zeros((SUBLANES, W_MIX), F32)

    u = ac_ref[...] * ax_ref[...]
    ext_ref[SUBLANES:, :] = u
    w = w_ref[...]
    conv = (w[0:1] * ext_ref[SUBLANES - 2:SUBLANES - 2 + tt, :]
            + w[1:2] * ext_ref[SUBLANES - 1:SUBLANES - 1 + tt, :] + w[2:3] * u)
    o_ref[...] = (ab_ref[...] * conv).astype(o_ref.dtype)
    ext_ref[0:SUBLANES, :] = ext_ref[tt:tt + SUBLANES, :]

    @pl.when(i == pl.num_programs(1) - 1)
    def _():
        st_ref[0] = ext_ref[tt + SUBLANES - (CONV_W - 1):tt + SUBLANES, :]


def _conv_prompt(proj, conv_w, nb, t):
    tt = T_SEQ
    nt = t // tt
    cb = lambda c: (lambda b, i: (b * nt + i, c // W_MIX))
    return pl.pallas_call(
        functools.partial(_conv_prompt_body, tt=tt),
        out_shape=(jax.ShapeDtypeStruct((proj.shape[0], W_MIX), BF16),
                   jax.ShapeDtypeStruct((nb, CONV_W - 1, W_MIX), F32)),
        grid=(nb, nt),
        in_specs=[pl.BlockSpec((tt, W_MIX), cb(C_AX)),
                  pl.BlockSpec((tt, W_MIX), cb(C_AB)),
                  pl.BlockSpec((tt, W_MIX), cb(C_AC)),
                  pl.BlockSpec((CONV_W, W_MIX), lambda b, i: (0, 0))],
        out_specs=(pl.BlockSpec((tt, W_MIX), lambda b, i: (b * nt + i, 0)),
                   pl.BlockSpec((1, CONV_W - 1, W_MIX), lambda b, i: (b, 0, 0))),
        scratch_shapes=[pltpu.VMEM((tt + SUBLANES, W_MIX), F32)],
        compiler_params=_cparams("parallel", "arbitrary"),
        name="conv_prompt",
    )(proj, proj, proj, conv_w)


def _conv_sample_body(ax_ref, ab_ref, ac_ref, st_ref, w_ref, o_ref, nst_ref, *, t):
    w = w_ref[...]
    u = [ac_ref[j] * ax_ref[j] for j in range(t)]
    ext = [st_ref[j] for j in range(CONV_W - 1)] + u
    for j in range(t):
        conv = w[0:1] * ext[j]
        for c in range(1, CONV_W):
            conv = conv + w[c:c + 1] * ext[j + c]
        o_ref[j] = ab_ref[j] * conv
    for j in range(CONV_W - 1):
        nst_ref[j] = ext[t + j]


def _conv_sample(ax, ab, ac, state, conv_w):
    t, nb, _ = ax.shape
    return pl.pallas_call(
        functools.partial(_conv_sample_body, t=t),
        out_shape=(jax.ShapeDtypeStruct((t, nb, W_MIX), F32),
                   jax.ShapeDtypeStruct((CONV_W - 1, nb, W_MIX), F32)),
        name="conv_sample",
    )(ax, ab, ac, state, conv_w)


def _pool_prompt_body(c_ref, pw_ref, ps_ref, o_ref, st_ref, ext_ref, *, tt):
    i = pl.program_id(1)
    hist = 2 * SUBLANES

    @pl.when(i == 0)
    def _():
        ext_ref[0:hist, :] = jnp.zeros((hist, W_MIX), F32)

    c = c_ref[...]
    ext_ref[hist:, :] = c
    pos = i * tt + lax.broadcasted_iota(jnp.int32, (tt, 1), 0)
    for g, win in enumerate(POOL_WINDOWS):
        cols = slice(g * GC, (g + 1) * GC)
        cg = c[:, cols]
        s = cg
        for j in range(1, win):
            s = s + ext_ref[hist - j:hist - j + tt, cols]
        cnt = jnp.minimum(win, pos + 1).astype(F32)
        pooled = s / cnt - cg
        og = jnp.dot(pooled.astype(BF16), pw_ref[g], preferred_element_type=F32)
        o_ref[:, cols] = (og * ps_ref[:, cols]).astype(o_ref.dtype)
    ext_ref[0:hist, :] = ext_ref[tt:tt + hist, :]

    @pl.when(i == pl.num_programs(1) - 1)
    def _():
        st_ref[0] = ext_ref[tt + hist - POOL_BUF:tt + hist, :]


def _pool_prompt(proj, pool_w, pool_scale, nb, t):
    tt = T_SEQ
    nt = t // tt
    return pl.pallas_call(
        functools.partial(_pool_prompt_body, tt=tt),
        out_shape=(jax.ShapeDtypeStruct((proj.shape[0], W_MIX), BF16),
                   jax.ShapeDtypeStruct((nb, POOL_BUF, W_MIX), F32)),
        grid=(nb, nt),
        in_specs=[pl.BlockSpec((tt, W_MIX), lambda b, i: (b * nt + i, C_CX // W_MIX)),
                  pl.BlockSpec((len(POOL_WINDOWS), GC, GC), lambda b, i: (0, 0, 0)),
                  pl.BlockSpec((1, W_MIX), lambda b, i: (0, 0))],
        out_specs=(pl.BlockSpec((tt, W_MIX), lambda b, i: (b * nt + i, 0)),
                   pl.BlockSpec((1, POOL_BUF, W_MIX), lambda b, i: (b, 0, 0))),
        scratch_shapes=[pltpu.VMEM((tt + 2 * SUBLANES, W_MIX), F32)],
        compiler_params=_cparams("parallel", "arbitrary"),
        name="pool_prompt",
    )(proj, pool_w, pool_scale)


def _pool_sample_body(c_ref, st_ref, pw_ref, ps_ref, o_ref, nst_ref, *, t, past_len):
    ext = [st_ref[j] for j in range(POOL_BUF)] + [c_ref[j] for j in range(t)]
    for g, win in enumerate(POOL_WINDOWS):
        cols = slice(g * GC, (g + 1) * GC)
        rows = []
        for j in range(t):
            s = ext[POOL_BUF + j][:, cols]
            for k in range(1, win):
                s = s + ext[POOL_BUF + j - k][:, cols]
            cnt = float(min(win, past_len + j + 1))
            rows.append(s / cnt - ext[POOL_BUF + j][:, cols])
        pooled = jnp.concatenate(rows, axis=0)
        og = jnp.dot(pooled.astype(BF16), pw_ref[g], preferred_element_type=F32) * ps_ref[:, cols]
        nb = og.shape[0] // t
        for j in range(t):
            o_ref[j, :, cols] = og[j * nb:(j + 1) * nb]
    for j in range(POOL_BUF):
        nst_ref[j] = ext[t + j]


def _pool_sample(c, state, pool_w, pool_scale, past_len):
    t, nb, _ = c.shape
    return pl.pallas_call(
        functools.partial(_pool_sample_body, t=t, past_len=past_len),
        out_shape=(jax.ShapeDtypeStruct((t, nb, W_MIX), F32),
                   jax.ShapeDtypeStruct((POOL_BUF, nb, W_MIX), F32)),
        name="pool_sample",
    )(c, state, pool_w, pool_scale)


def _upper_ones(n):
    r = lax.broadcasted_iota(jnp.int32, (n, n), 0)
    c = lax.broadcasted_iota(jnp.int32, (n, n), 1)
    return (r <= c).astype(F32)


def _cumsum_body(lf_ref, o_ref, carry_ref, *, tc):
    @pl.when(pl.program_id(1) == 0)
    def _():
        carry_ref[...] = jnp.zeros_like(carry_ref)

    f = jnp.dot(lf_ref[...], _upper_ones(tc), preferred_element_type=F32,
                precision=lax.Precision.HIGHEST) + carry_ref[...]
    o_ref[...] = f * LOG2E
    carry_ref[...] = f[:, tc - 1:tc]


def _cumsum_prompt(lf_t, nb, t):
    tc = T_SEQ
    nt = t // tc
    return pl.pallas_call(
        functools.partial(_cumsum_body, tc=tc),
        out_shape=jax.ShapeDtypeStruct((SUBLANES, nb * t), F32),
        grid=(nb, nt),
        in_specs=[pl.BlockSpec((SUBLANES, tc), lambda b, i: (0, b * nt + i))],
        out_specs=pl.BlockSpec((SUBLANES, tc), lambda b, i: (0, b * nt + i)),
        scratch_shapes=[pltpu.VMEM((SUBLANES, 1), F32)],
        compiler_params=_cparams("parallel", "arbitrary"),
        name="cumsum_prompt",
    )(lf_t)


def _diff_lambda(dl_ref, lam_init):
    dl = dl_ref[...]
    a = jnp.sum(dl[0:1] * dl[1:2], axis=-1, keepdims=True)
    b = jnp.sum(dl[2:3] * dl[3:4], axis=-1, keepdims=True)
    return jnp.exp(a) - jnp.exp(b) + lam_init


def _half_masks(shape):
    lane = lax.broadcasted_iota(jnp.int32, shape, len(shape) - 1)
    return lane < DQK_D, lane >= DQK_D


def _softmax_step(s, v, m_ref, l_ref, acc_ref):
    m_prev = m_ref[...]
    m_new = jnp.maximum(m_prev, jnp.max(s, axis=-1, keepdims=True))
    alpha = jnp.exp2(m_prev - m_new)
    p = jnp.exp2(s - m_new)
    l_ref[...] = alpha * l_ref[...] + jnp.sum(p, axis=-1, keepdims=True)
    acc_ref[...] = alpha * acc_ref[...] + jnp.dot(p.astype(BF16), v, preferred_element_type=F32)
    m_ref[...] = m_new


def _attn_prompt_body(*refs, diff, lam_init, tq):
    n_soft = 2 if diff else 1
    n_state = n_soft * H_ATT
    n_in = 5 if diff else 4
    if diff:
        q_ref, k_ref, v_ref, dl_ref, sub_ref, o_ref = refs[:n_in + 1]
    else:
        q_ref, k_ref, v_ref, f_ref, o_ref = refs[:n_in + 1]
    m_refs = refs[n_in + 1:n_in + 1 + n_state]
    l_refs = refs[n_in + 1 + n_state:n_in + 1 + 2 * n_state]
    acc_refs = refs[n_in + 1 + 2 * n_state:]
    qi = pl.program_id(1)
    ki = pl.program_id(2)
    scale = (DIFF_SCALE if diff else FOX_SCALE) * LOG2E

    @pl.when(ki == 0)
    def _():
        for m_ref, l_ref, acc_ref in zip(m_refs, l_refs, acc_refs):
            m_ref[...] = jnp.full(m_ref.shape, NEG_INF, F32)
            l_ref[...] = jnp.zeros(l_ref.shape, F32)
            acc_ref[...] = jnp.zeros(acc_ref.shape, F32)

    def block(masked):
        if masked:
            causal = (lax.broadcasted_iota(jnp.int32, (tq, tq), 1)
                      <= lax.broadcasted_iota(jnp.int32, (tq, tq), 0))
        def scores(h, n):
            cs = slice(h * DH, (h + 1) * DH)
            q = q_ref[:, cs] * scale
            if diff:
                q = jnp.where(_half_masks(q.shape)[n], q, 0.0)
            s = _nt_dot(q.astype(BF16), k_ref[:, cs].astype(BF16))
            if not diff:
                s = s - f_ref[h:h + 1, :]
            if masked:
                s = jnp.where(causal, s, NEG_INF)
            return s

        items = [(h, n) for h in range(H_ATT) for n in range(n_soft)]
        s_next = scores(*items[0])
        for pos, (h, n) in enumerate(items):
            s = s_next
            if pos + 1 < len(items):
                s_next = scores(*items[pos + 1])
            v = v_ref[:, h * DH:(h + 1) * DH].astype(BF16)
            idx = n * H_ATT + h
            _softmax_step(s, v, m_refs[idx], l_refs[idx], acc_refs[idx])

    @pl.when(ki < qi)
    def _():
        block(False)

    @pl.when(ki == qi)
    def _():
        block(True)
        if diff:
            lam = _diff_lambda(dl_ref, lam_init)
        for h in range(H_ATT):
            o = acc_refs[h][...] / l_refs[h][...]
            if diff:
                o = o - lam * (acc_refs[H_ATT + h][...] / l_refs[H_ATT + h][...])
                o = _rms(o, sub_ref[...]) * (1.0 - lam_init)
            o_ref[:, h * DH:(h + 1) * DH] = o.astype(o_ref.dtype)


def _attn_prompt(proj, nb, t, c_q, c_k, c_v, *, f_cum=None, diff_lambda=None, diff_subln=None,
                 lam_init=0.0):
    diff = f_cum is None
    tq = TQ
    nt = t // tq
    wide = H_ATT * DH
    n_state = (2 if diff else 1) * H_ATT
    in_specs = [pl.BlockSpec((tq, wide), lambda b, qi, ki: (b * nt + qi, c_q // wide)),
                pl.BlockSpec((tq, wide), lambda b, qi, ki: (b * nt + jnp.minimum(ki, qi), c_k // wide)),
                pl.BlockSpec((tq, wide), lambda b, qi, ki: (b * nt + jnp.minimum(ki, qi), c_v // wide))]
    args = [proj, proj, proj]
    if diff:
        in_specs += [pl.BlockSpec((4, DQK_D), lambda b, qi, ki: (0, 0)),
                     pl.BlockSpec((1, DH), lambda b, qi, ki: (0, 0))]
        args += [diff_lambda, diff_subln.reshape(1, DH)]
    else:
        in_specs += [pl.BlockSpec((SUBLANES, tq), lambda b, qi, ki: (0, b * nt + jnp.minimum(ki, qi)))]
        args += [f_cum]
    return pl.pallas_call(
        functools.partial(_attn_prompt_body, diff=diff, lam_init=lam_init, tq=tq),
        out_shape=jax.ShapeDtypeStruct((proj.shape[0], wide), BF16),
        grid=(nb, nt, nt),
        in_specs=in_specs,
        out_specs=pl.BlockSpec((tq, wide), lambda b, qi, ki: (b * nt + qi, 0)),
        scratch_shapes=([pltpu.VMEM((tq, 1), F32)] * (2 * n_state) + [pltpu.VMEM((tq, DH), F32)] * n_state),
        compiler_params=_cparams("parallel", "parallel", "arbitrary"),
        name="attn_prompt_diff" if diff else "attn_prompt_fox",
    )(*args)


def _attn_sample_body(*refs, diff, lam_init, t):
    np_ = PAGES_PER_STEP
    pt_ref, q_ref, kn_ref, vn_ref = refs[:4]
    if diff:
        dl_ref, sub_ref = refs[4:6]
        pages = refs[6:6 + np_]
        o_ref, m_ref, l_ref, acc_ref = refs[6 + np_:]
    else:
        lfn_ref = refs[4]
        pages = refs[5:5 + np_]
        lf_pages = refs[5 + np_:5 + 2 * np_]
        o_ref, m_ref, l_ref, acc_ref, carry_ref = refs[5 + 2 * np_:]
    del pt_ref
    step = pl.program_id(1)
    last = pl.num_programs(1) - 1
    r = 2 * t if diff else t
    scale = (DIFF_SCALE if diff else FOX_SCALE) * LOG2E
    row_stride = 2 * H_ATT

    @pl.when(step == 0)
    def _():
        m_ref[...] = jnp.full(m_ref.shape, NEG_INF, F32)
        l_ref[...] = jnp.zeros(l_ref.shape, F32)
        acc_ref[...] = jnp.zeros(acc_ref.shape, F32)
        if not diff:
            carry_ref[...] = jnp.zeros(carry_ref.shape, F32)

    q_all = q_ref[0] * scale

    def head_q(h):
        qh = q_all[:, h * DH:(h + 1) * DH]
        if diff:
            lo, hi = _half_masks(qh.shape)
            qh = jnp.concatenate([jnp.where(lo, qh, 0.0), jnp.where(hi, qh, 0.0)], axis=0)
        return qh

    def per_head(x, first, stride, rows):
        return jnp.concatenate(
            [jnp.broadcast_to(x[first + h * stride:first + h * stride + 1], (rows, x.shape[1]))
             for h in range(H_ATT)], axis=0)

    def cache_rows(first):
        return jnp.concatenate([pg[pl.ds(first, PAGE_SIZE, stride=row_stride), :] for pg in pages],
                               axis=0).astype(BF16)

    if not diff:
        lf = jnp.concatenate([lp[pl.ds(h, 1), :] for h in range(H_ATT) for lp in lf_pages], axis=0)
        n_lf = H_ATT * np_
        ra = lax.broadcasted_iota(jnp.int32, (n_lf, n_lf), 0)
        rb = lax.broadcasted_iota(jnp.int32, (n_lf, n_lf), 1)
        earlier = ((ra // np_ == rb // np_) & (rb < ra)).astype(F32)
        prev = jnp.dot(earlier, lf, preferred_element_type=F32, precision=lax.Precision.HIGHEST)
        rr = lax.broadcasted_iota(jnp.int32, (2 * PAGE_SIZE, PAGE_SIZE), 0)
        cc = lax.broadcasted_iota(jnp.int32, (2 * PAGE_SIZE, PAGE_SIZE), 1)
        tri_ones = ((rr <= cc) | (rr >= PAGE_SIZE)).astype(F32)
        f = jnp.dot(jnp.concatenate([lf, prev], axis=1), tri_ones, preferred_element_type=F32,
                    precision=lax.Precision.HIGHEST) + carry_ref[...]
        carry_ref[...] = per_head(f[:, PAGE_SIZE - 1:PAGE_SIZE], np_ - 1, np_, np_)
        f = f * LOG2E
        bias = jnp.concatenate(
            [jnp.broadcast_to(jnp.concatenate([f[h * np_ + i:h * np_ + i + 1] for i in range(np_)], axis=1),
                              (r, np_ * PAGE_SIZE)) for h in range(H_ATT)], axis=0)
    s = jnp.concatenate([_nt_dot(head_q(h).astype(BF16), cache_rows(h)) for h in range(H_ATT)],
                        axis=0)
    if not diff:
        s = s - bias
    m_prev = m_ref[...]
    m_new = jnp.maximum(m_prev, jnp.max(s, axis=-1, keepdims=True))
    alpha = jnp.exp2(m_prev - m_new)
    p = jnp.exp2(s - m_new)
    l_ref[...] = alpha * l_ref[...] + jnp.sum(p, axis=-1, keepdims=True)
    pv = jnp.concatenate([jnp.dot(p[h * r:(h + 1) * r].astype(BF16), cache_rows(H_ATT + h),
                                  preferred_element_type=F32) for h in range(H_ATT)], axis=0)
    acc_ref[...] = alpha * acc_ref[...] + pv
    m_ref[...] = m_new

    @pl.when(step == last)
    def _():
        qrow = lax.broadcasted_iota(jnp.int32, (H_ATT * r, 1), 0) % t
        qh = [head_q(h) for h in range(H_ATT)]
        kn = kn_ref[0]
        vn = vn_ref[0]
        m = m_ref[...]
        l = l_ref[...]
        acc = acc_ref[...]
        if not diff:
            off = per_head(carry_ref[...], 0, np_, r)
            lfn = lfn_ref[0]
        for j in range(t):
            sj = jnp.concatenate(
                [jnp.sum(qh[h] * kn[j:j + 1, h * DH:(h + 1) * DH], axis=-1, keepdims=True)
                 for h in range(H_ATT)], axis=0)
            if not diff:
                off = off + jnp.concatenate(
                    [jnp.broadcast_to(lfn[h:h + 1, j:j + 1], (r, 1)) for h in range(H_ATT)], axis=0)
                sj = sj - off * LOG2E
            sj = jnp.where(qrow >= j, sj, NEG_INF)
            m_new = jnp.maximum(m, sj)
            alpha = jnp.exp2(m - m_new)
            pj = jnp.exp2(sj - m_new)
            l = alpha * l + pj
            vj = jnp.concatenate([jnp.broadcast_to(vn[j:j + 1, h * DH:(h + 1) * DH], (r, DH))
                                  for h in range(H_ATT)], axis=0)
            acc = alpha * acc + pj * vj
            m = m_new
        o_all = acc / l
        if diff:
            lam = _diff_lambda(dl_ref, lam_init)
        for h in range(H_ATT):
            o = o_all[h * r:(h + 1) * r]
            if diff:
                o = o[0:t] - lam * o[t:2 * t]
                o = _rms(o, sub_ref[...]) * (1.0 - lam_init)
            o_ref[0, :, h * DH:(h + 1) * DH] = o


def _attn_sample(proj_s, cache, page_table, layer, c_q, c_k, c_v, *, lf_new=None, cache_lf=None,
                 diff_lambda=None, diff_subln=None, lam_init=0.0):
    diff = cache_lf is None
    nb, t, _ = proj_s.shape
    n_pages = page_table.shape[1]
    np_ = PAGES_PER_STEP
    n_steps = n_pages // np_
    wide = H_ATT * DH
    r = 2 * t if diff else t

    def page_map(i):
        return lambda b, s, pt: (layer, pt[b * n_pages + s * np_ + i], 0, 0)

    in_specs = [pl.BlockSpec((1, t, wide), lambda b, s, pt: (b, 0, c_q // wide)),
                pl.BlockSpec((1, t, wide), lambda b, s, pt: (b, 0, c_k // wide)),
                pl.BlockSpec((1, t, wide), lambda b, s, pt: (b, 0, c_v // wide))]
    args = [proj_s, proj_s, proj_s]
    if diff:
        in_specs += [pl.BlockSpec((4, DQK_D), lambda b, s, pt: (0, 0)),
                     pl.BlockSpec((1, DH), lambda b, s, pt: (0, 0))]
        args += [diff_lambda, diff_subln.reshape(1, DH)]
    else:
        in_specs += [pl.BlockSpec((1, SUBLANES, t), lambda b, s, pt: (b, 0, 0))]
        args += [lf_new]
    in_specs += [pl.BlockSpec((None, None, PAGE_SIZE * 2 * H_ATT, DH), page_map(i)) for i in range(np_)]
    args += [cache] * np_
    scratch = [pltpu.VMEM((H_ATT * r, 1), F32), pltpu.VMEM((H_ATT * r, 1), F32),
               pltpu.VMEM((H_ATT * r, DH), F32)]
    if not diff:
        in_specs += [pl.BlockSpec((None, None, H_ATT, PAGE_SIZE), page_map(i)) for i in range(np_)]
        args += [cache_lf] * np_
        scratch += [pltpu.VMEM((H_ATT * np_, 1), F32)]
    return pl.pallas_call(
        functools.partial(_attn_sample_body, diff=diff, lam_init=lam_init, t=t),
        out_shape=jax.ShapeDtypeStruct((nb, t, wide), F32),
        grid_spec=pltpu.PrefetchScalarGridSpec(
            num_scalar_prefetch=1,
            grid=(nb, n_steps),
            in_specs=in_specs,
            out_specs=pl.BlockSpec((1, t, wide), lambda b, s, pt: (b, 0, 0)),
            scratch_shapes=scratch),
        compiler_params=_cparams("parallel", "arbitrary"),
        name="attn_sample_diff" if diff else "attn_sample_fox",
    )(page_table.reshape(-1), *args)


def _kv_state_body(*refs, n_groups):
    x_ref, o_ref = refs[0], refs[-1]
    tm = x_ref.shape[0]
    for c in range(n_groups):
        o_ref[pl.ds(c, tm, stride=n_groups), :] = x_ref[:, c * DH:(c + 1) * DH]


def _kv_state(src, col0, n_rows, layer, prev=None):
    n_groups = 2 * H_ATT
    wide = n_groups * DH
    tm = T_SEQ
    out_shape = jax.ShapeDtypeStruct((DEPTH, n_rows * n_groups, DH), F32)
    in_specs = [pl.BlockSpec((tm, wide), lambda i: (i, col0 // wide))]
    args = [src]
    aliases = {}
    if prev is not None:
        in_specs.append(pl.BlockSpec(memory_space=pl.ANY))
        args.append(prev)
        aliases = {1: 0}
    return pl.pallas_call(
        functools.partial(_kv_state_body, n_groups=n_groups),
        out_shape=out_shape,
        grid=(n_rows // tm,),
        in_specs=in_specs,
        out_specs=pl.BlockSpec((None, tm * n_groups, DH), lambda i: (layer, i, 0)),
        input_output_aliases=aliases,
        compiler_params=_cparams("parallel"),
        name="kv_state",
    )(*args)


def _new_expert(te_ref):
    i = pl.program_id(1)
    return (i == 0) | (te_ref[i] != te_ref[jnp.maximum(i - 1, 0)])


def _moe_in_body(te_ref, nt_ref, x_ref, wg_ref, wu_ref, o_ref, wgb_ref, wub_ref):
    _cast_when(_new_expert(te_ref), [(wg_ref, wgb_ref), (wu_ref, wub_ref)])

    @pl.when(pl.program_id(1) < nt_ref[0])
    def _():
        x = x_ref[...]
        g = jnp.dot(x, wgb_ref[...], preferred_element_type=F32)
        u = jnp.dot(x, wub_ref[...], preferred_element_type=F32)
        o_ref[...] = (g * jax.nn.sigmoid(g) * u).astype(o_ref.dtype)

    @pl.when(pl.program_id(1) >= nt_ref[0])
    def _():
        o_ref[...] = jnp.zeros(o_ref.shape, o_ref.dtype)


def _moe_in(xs, w_in, tile_expert, n_tiles, wl):
    p, d = xs.shape
    f = w_in.shape[3] // 2
    nj = f // TN_E
    return pl.pallas_call(
        _moe_in_body,
        out_shape=jax.ShapeDtypeStruct((p, f), BF16),
        grid_spec=pltpu.PrefetchScalarGridSpec(
            num_scalar_prefetch=2,
            grid=(nj, p // TM_E),
            in_specs=[pl.BlockSpec((TM_E, d), lambda j, i, te, nt: (i, 0)),
                      pl.BlockSpec((None, None, d, TN_E), lambda j, i, te, nt: (wl, te[i], 0, j)),
                      pl.BlockSpec((None, None, d, TN_E), lambda j, i, te, nt: (wl, te[i], 0, nj + j))],
            out_specs=pl.BlockSpec((TM_E, TN_E), lambda j, i, te, nt: (i, j)),
            scratch_shapes=[pltpu.VMEM((d, TN_E), BF16), pltpu.VMEM((d, TN_E), BF16)]),
        compiler_params=_cparams("parallel", "arbitrary"),
        name="moe_in",
    )(tile_expert, n_tiles, xs, w_in, w_in)


def _moe_out_body(te_ref, nt_ref, a_ref, w_ref, o_ref, wb_ref):
    _cast_when(_new_expert(te_ref), [(w_ref, wb_ref)])

    @pl.when(pl.program_id(1) < nt_ref[0])
    def _():
        o_ref[...] = jnp.dot(a_ref[...], wb_ref[...], preferred_element_type=F32)

    @pl.when(pl.program_id(1) >= nt_ref[0])
    def _():
        o_ref[...] = jnp.zeros(o_ref.shape, o_ref.dtype)


def _moe_out(act, w_out, tile_expert, n_tiles, tn, wl):
    p, f = act.shape
    d = w_out.shape[3]
    return pl.pallas_call(
        _moe_out_body,
        out_shape=jax.ShapeDtypeStruct((p, d), F32),
        grid_spec=pltpu.PrefetchScalarGridSpec(
            num_scalar_prefetch=2,
            grid=(d // tn, p // TM_E),
            in_specs=[pl.BlockSpec((TM_E, f), lambda j, i, te, nt: (i, 0)),
                      pl.BlockSpec((None, None, f, tn), lambda j, i, te, nt: (wl, te[i], 0, j))],
            out_specs=pl.BlockSpec((TM_E, tn), lambda j, i, te, nt: (i, j)),
            scratch_shapes=[pltpu.VMEM((f, tn), BF16)]),
        compiler_params=_cparams("parallel", "arbitrary"),
        name="moe_out",
    )(tile_expert, n_tiles, act, w_out)


def _combine_body(x_ref, g0_ref, g1_ref, w_ref, gn_ref, o_ref, *, final_norm):
    w = w_ref[...]
    y = x_ref[...] + w[:, 0:1] * g0_ref[...] + w[:, 1:2] * g1_ref[...]
    if final_norm:
        y = _rms(y, gn_ref[...])
    o_ref[...] = y


def _combine(x, g, wts, g_final=None):
    n, d = x.shape
    tm = TM // 2
    final_norm = g_final is not None
    gn = (g_final if final_norm else jnp.ones((d,), F32)).reshape(1, d)
    return pl.pallas_call(
        functools.partial(_combine_body, final_norm=final_norm),
        out_shape=jax.ShapeDtypeStruct((n, d), F32),
        grid=(n // tm,),
        in_specs=[pl.BlockSpec((tm, d), lambda i: (i, 0)),
                  pl.BlockSpec((None, tm, d), lambda i: (0, i, 0)),
                  pl.BlockSpec((None, tm, d), lambda i: (1, i, 0)),
                  pl.BlockSpec((tm, LANES), lambda i: (i, 0)),
                  pl.BlockSpec((1, d), lambda i: (0, 0))],
        out_specs=pl.BlockSpec((tm, d), lambda i: (i, 0)),
        compiler_params=_cparams("parallel"),
        name="moe_combine",
    )(x, g, g, wts, gn)


def _moe(x, g_norm, router, w_in, w_out, wl, g_final=None):
    n, d = x.shape
    router_pad = jnp.pad(router, ((0, 0), (0, LANES - N_EXP)))
    h, ids, wts, rank, cnt = _norm_route(x, g_norm, router_pad)
    p_total = n * TOP_K + N_EXP * TM_E
    counts = cnt[0, :N_EXP].astype(jnp.int32)
    padded = (counts + TM_E - 1) // TM_E * TM_E
    ends = jnp.cumsum(padded)
    starts = ends - padded
    ids_k = ids[:, :TOP_K].T
    group_start = jnp.sum(jnp.where(ids_k[:, :, None] == jnp.arange(N_EXP, dtype=jnp.int32), starts, 0), axis=-1)
    dest = group_start + rank[:, :TOP_K].T
    tok = jnp.broadcast_to(jnp.arange(n, dtype=jnp.int32), (TOP_K, n))
    src_tok = jnp.zeros((p_total,), jnp.int32).at[dest.reshape(-1)].set(tok.reshape(-1))
    n_tiles = (ends[-1] // TM_E).astype(jnp.int32).reshape(1)
    tile_start = jnp.arange(p_total // TM_E, dtype=jnp.int32) * TM_E
    tile_expert = jnp.minimum(jnp.sum((tile_start[:, None] >= ends[None, :]).astype(jnp.int32), axis=1),
                              N_EXP - 1)
    last_valid = jnp.maximum(n_tiles[0] - 1, 0)
    tile_expert = jnp.where(tile_start // TM_E < n_tiles[0], tile_expert, tile_expert[last_valid])

    xs = jnp.take(h, src_tok, axis=0, mode="clip")
    act = _moe_in(xs, w_in, tile_expert, n_tiles, wl)
    ys = _moe_out(act, w_out, tile_expert, n_tiles, 512, wl)
    g = jnp.take(ys, dest, axis=0, mode="clip")
    return _combine(x, g, wts, g_final)


def _token_mixers(x, layer, lam_init, nb_p, t_p, nb_s, t_s, caches, states, page_table, lp, kv_bufs):
    (g_mix, w_in_t, w_in_hi_t, b_forget, conv_w, pool_w, pool_scale, diff_lambda, diff_subln,
     w_branch, w_gate, b_gate, w_o) = lp
    cache_fox, cache_lf_t, cache_diff = caches
    state_conv, state_pool = states
    n_p = nb_p * t_p
    past_len = page_table.shape[1] * PAGE_SIZE

    wf_t = jnp.pad(w_in_t[N_LO:N_LO + H_ATT, layer, :], ((0, SUBLANES - H_ATT), (0, 0)))
    bf = jnp.pad(b_forget, (0, SUBLANES - H_ATT)).reshape(SUBLANES, 1)

    h, lf_t = _norm_forget(x, g_mix, wf_t, bf)
    lo = _matmul_t(h, w_in_t, 512, layer, n_cols=N_LO)
    hi = _matmul_t(h, w_in_hi_t, 512, layer)

    out_a, conv_p = _conv_prompt(lo, conv_w, nb_p, t_p)
    pool_w_b = pool_w.astype(BF16)
    pool_scale2 = pool_scale.reshape(1, W_MIX)
    out_c, pool_p = _pool_prompt(hi, pool_w_b, pool_scale2, nb_p, t_p)
    f_cum = _cumsum_prompt(lf_t, nb_p, t_p)
    out_b = _attn_prompt(lo, nb_p, t_p, C_FQ, C_FK, C_FV, f_cum=f_cum)
    out_d = _attn_prompt(hi, nb_p, t_p, C_DQ, C_DK, C_DV, diff_lambda=diff_lambda,
                         diff_subln=diff_subln, lam_init=lam_init)

    lo_s = lo[n_p:].reshape(nb_s, t_s, N_LO)
    hi_s = hi[n_p:].reshape(nb_s, t_s, N_HI)
    tmaj = lambda a, c: a[:, :, c:c + W_MIX].transpose(1, 0, 2)
    out_a_s, conv_s = _conv_sample(tmaj(lo_s, C_AX), tmaj(lo_s, C_AB), tmaj(lo_s, C_AC),
                                   state_conv.transpose(1, 0, 2), conv_w)
    out_c_s, pool_s = _pool_sample(tmaj(hi_s, C_CX), state_pool.transpose(1, 0, 2), pool_w_b, pool_scale2,
                                   past_len)
    lf_new = lf_t[:, n_p:].reshape(SUBLANES, nb_s, t_s).transpose(1, 0, 2)
    out_b_s = _attn_sample(lo_s, cache_fox, page_table, layer, C_FQ, C_FK, C_FV,
                           lf_new=lf_new, cache_lf=cache_lf_t)
    out_d_s = _attn_sample(hi_s, cache_diff, page_table, layer, C_DQ, C_DK, C_DV,
                           diff_lambda=diff_lambda, diff_subln=diff_subln, lam_init=lam_init)

    def with_sample(p, s):
        return lax.dynamic_update_slice(p, s.reshape(nb_s * t_s, W_MIX).astype(BF16), (n_p, 0))

    branches = [with_sample(out_a, out_a_s.transpose(1, 0, 2)), with_sample(out_b, out_b_s),
                with_sample(out_c, out_c_s.transpose(1, 0, 2)), with_sample(out_d, out_d_s)]
    merged = _merge(h, branches, w_gate, w_branch, b_gate.reshape(1, N_BRANCH * D_MODEL), 256, layer)
    x = _matmul(merged, w_o, 1024, res=x, wl=layer)

    lf_rows = lf_t[:H_ATT].T
    kv = lambda a, rows, c, nb, t: a[rows, c:c + 2 * W_MIX].reshape(nb, t, 2, H_ATT, DH)
    rp, rs = slice(0, n_p), slice(n_p, None)
    kv_bufs = (_kv_state(lo, C_FK, n_p, layer, kv_bufs[0]), _kv_state(hi, C_DK, n_p, layer, kv_bufs[1]))
    st_p = (None, lf_rows[rp].reshape(nb_p, t_p, H_ATT), None, conv_p, pool_p)
    st_s = (kv(lo, rs, C_FK, nb_s, t_s), lf_rows[rs].reshape(nb_s, t_s, H_ATT), kv(hi, rs, C_DK, nb_s, t_s),
            conv_s.transpose(1, 0, 2), pool_s.transpose(1, 0, 2))
    return x, st_p, st_s, kv_bufs


def kernel(x_prompt, x_sample, cache_fox_kv, cache_fox_logf, cache_diff_kv, state_conv, state_pool,
           page_table, norm_mix, w_in, b_forget, conv_w, pool_w, pool_scale, diff_lambda, diff_subln,
           w_branch, w_gate, b_gate, w_o, norm_ffn, ffn_w_in, ffn_w_out, moe_router, moe_w_in,
           moe_w_out, norm_final):
    nb_p, t_p, d = x_prompt.shape
    nb_s, t_s, _ = x_sample.shape
    n_p = nb_p * t_p
    depth, n_phys = cache_fox_kv.shape[:2]
    assert d == D_MODEL and depth == DEPTH
    assert (n_p + nb_s * t_s) % TM == 0 and t_p % T_SEQ == 0 and t_p % TQ == 0
    assert page_table.shape[1] % PAGES_PER_STEP == 0

    x = jnp.concatenate([x_prompt.reshape(n_p, d), x_sample.reshape(nb_s * t_s, d)], axis=0)
    caches = (cache_fox_kv.reshape(depth, n_phys, PAGE_SIZE * 2 * H_ATT, DH),
              cache_fox_logf.transpose(0, 1, 3, 2),
              cache_diff_kv.reshape(depth, n_phys, PAGE_SIZE * 2 * H_ATT, DH))
    w_in_t = w_in.transpose(2, 0, 1)
    w_in_hi_t = w_in_t[N_LO + H_ATT:]
    st_p, st_s = [], []
    kv_bufs = (None, None)
    for l in range(depth):
        lam_init = 0.8 - 0.6 * math.exp(-0.3 * l)
        lp = (norm_mix[l], w_in_t, w_in_hi_t, b_forget[l], conv_w[l], pool_w[l], pool_scale[l], diff_lambda[l],
              diff_subln[l], w_branch, w_gate, b_gate[l], w_o)
        x, sp, ss, kv_bufs = _token_mixers(x, l, lam_init, nb_p, t_p, nb_s, t_s, caches,
                                           (state_conv[l], state_pool[l]), page_table, lp, kv_bufs)
        st_p.append(sp)
        st_s.append(ss)
        j = l // 2
        if l % 2 == 0:
            hf = _norm_plain(x, norm_ffn[l], BF16)
            act = _swiglu_in(hf, ffn_w_in, 512, wl=j)
            x = _matmul(act, ffn_w_out, 512, res=x, wl=j)
        else:
            x = _moe(x, norm_ffn[l], moe_router[j], moe_w_in, moe_w_out, j,
                     g_final=norm_final if l == depth - 1 else None)
    y = x if depth % 2 == 0 else _norm_plain(x, norm_final, F32)
    outs = [y[:n_p].reshape(nb_p, t_p, d), y[n_p:].reshape(nb_s, t_s, d)]
    kv_shape = (depth, nb_p, t_p, 2, H_ATT, DH)
    for sts in (st_p, st_s):
        for k in range(5):
            if sts[0][k] is None:
                outs.append(kv_bufs[k // 2].reshape(kv_shape))
            else:
                outs.append(jnp.stack([s[k] for s in sts]))
    return tuple(outs)
```
